```python
import jax
import jax.numpy as jnp
from jax import lax
import numpy as np

D_MODEL = 1024
BATCH = 4
SEQ = 4096
DEPTH = 4
DEC_BATCH = 128
DEC_SEQ = 4
PAST_LEN = 2048
PAGE_SIZE = 128

HEAD_DIM = 64
N_HEADS = D_MODEL // 128
N_KV_HEADS = max(2, N_HEADS // 4)
Q_PER_KV = N_HEADS // N_KV_HEADS
ATT_WIDTH = N_HEADS * HEAD_DIM
KV_WIDTH = N_KV_HEADS * HEAD_DIM
N_KV_STREAMS = 6
N_CACHED_STREAMS = 4
CMP_LEN = 32
CMP_STRIDE = 16
CMP_HIDDEN = 2 * HEAD_DIM
SEL_BLOCK = 64
SEL_TOPK = 16
N_LOCAL_BLOCKS = 2
WINDOW = 512
Q_BLOCK = 128
FORCED_SCORE = 1e4
POOL_WINDOWS = (2, 4, 8, 16)
N_POOL_GROUPS = 4
POOL_WIDTH = D_MODEL // 4
POOL_GROUP = POOL_WIDTH // N_POOL_GROUPS
POOL_MAX = 16
CONV_WIDTH = 3
CONV_DIM = D_MODEL // 4
D_FF = ((8 * D_MODEL // 3 + 127) // 128) * 128
ROPE_THETA = 10000.0
EPS = 1e-6
IN_SPLITS = (POOL_WIDTH, ATT_WIDTH, N_KV_STREAMS * KV_WIDTH, 3 * N_HEADS, CONV_DIM, CONV_DIM, CONV_DIM, D_MODEL, D_MODEL, D_MODEL)
IN_WIDTH = sum(IN_SPLITS)

kernel_name = 'nsa_pool_conv_hybrid_step'


def rmsnorm(x, g):
    xf = x.astype(jnp.float32)
    y = xf * lax.rsqrt(jnp.mean(xf * xf, axis=-1, keepdims=True) + EPS)
    return (y * g.astype(jnp.float32)).astype(x.dtype)


def rope(x, pos):
    half = HEAD_DIM // 2
    inv = ROPE_THETA ** (-jnp.arange(half, dtype=jnp.float32) / half)
    ang = pos.astype(jnp.float32)[:, None] * inv[None, :]
    cos = jnp.cos(ang)[:, None, :]
    sin = jnp.sin(ang)[:, None, :]
    xf = x.astype(jnp.float32)
    x1, x2 = xf[..., :half], xf[..., half:]
    return jnp.concatenate([x1 * cos - x2 * sin, x2 * cos + x1 * sin], axis=-1).astype(x.dtype)


def split_columns(z):
    parts, off = [], 0
    for w in IN_SPLITS:
        parts.append(z[..., off:off + w])
        off += w
    return parts


def query_blocks(S):
    c = Q_BLOCK if S % Q_BLOCK == 0 else S
    return c, S // c


def causal_dwconv(u, prefix, w):
    S = u.shape[1]
    ext = jnp.concatenate([prefix, u], axis=1)
    out = w[0] * ext[:, 0:S]
    for k in range(1, CONV_WIDTH):
        out = out + w[k] * ext[:, k:k + S]
    return out, ext[:, ext.shape[1] - (CONV_WIDTH - 1):]


def pool_mixer(u, prefix, pos0, pool_w, pool_scale):
    B, S, C = u.shape
    P = prefix.shape[1]
    ext = jnp.concatenate([prefix, u], axis=1).astype(jnp.float32)
    cs = jnp.concatenate([jnp.zeros((B, 1, C), jnp.float32), jnp.cumsum(ext, axis=1)], axis=1)
    end = cs[:, P + 1:]
    t_abs = (pos0 + jnp.arange(S)).astype(jnp.float32)
    means = []
    for g, w in enumerate(POOL_WINDOWS):
        cols = slice(g * POOL_GROUP, (g + 1) * POOL_GROUP)
        start = cs[:, P + 1 - w:P + 1 - w + S, cols]
        count = jnp.minimum(float(w), t_abs + 1.0)[:, None]
        means.append((end[..., cols] - start) / count)
    pooled = jnp.concatenate(means, axis=-1) - ext[:, P:]
    y = jnp.einsum('bsgc,gcd->bsgd', pooled.reshape(B, S, N_POOL_GROUPS, POOL_GROUP), pool_w.astype(jnp.float32))
    y = y.reshape(B, S, C) * pool_scale.astype(jnp.float32)
    return y.astype(u.dtype), ext[:, ext.shape[1] - P:].astype(u.dtype)


def compress_blocks(rows, pe, w1, w2):
    B, T = rows.shape[0], rows.shape[1]
    n_cmp = (T - CMP_LEN) // CMP_STRIDE + 1
    idx = jnp.arange(n_cmp)[:, None] * CMP_STRIDE + jnp.arange(CMP_LEN)[None, :]
    blk = rows[:, idx] + pe[:, None, :]
    blk = jnp.moveaxis(blk, 3, 2).reshape(B, n_cmp, N_KV_HEADS, CMP_LEN * HEAD_DIM)
    return jax.nn.gelu(blk @ w1) @ w2


def nsa_cmp_sel(q, pos_q, kc, vc, pos_c, ks_blocks, vs_blocks):
    B, c = q.shape[0], q.shape[1]
    n_cmp = kc.shape[1]
    n_sel = ks_blocks.shape[2]
    scale = HEAD_DIM ** -0.5
    qg = q.reshape(B, c, N_KV_HEADS, Q_PER_KV, HEAD_DIM)
    s = jnp.einsum('bcgrd,bngd->bgrcn', qg, kc).astype(jnp.float32) * scale
    ok_c = pos_c[None, :] <= pos_q[:, None]
    s = jnp.where(ok_c, s, -jnp.inf)
    m = jnp.max(s, axis=-1, keepdims=True)
    m = jnp.where(jnp.isfinite(m), m, 0.0)
    e = jnp.exp(s - m)
    p_cmp = e / jnp.maximum(jnp.sum(e, axis=-1, keepdims=True), 1.0)
    o_cmp = jnp.einsum('bgrcn,bngd->bcgrd', p_cmp.astype(vc.dtype), vc)
    c_start = jnp.arange(n_cmp) * CMP_STRIDE
    j_start = jnp.arange(n_sel) * SEL_BLOCK
    cover = ((c_start[:, None] < j_start[None, :] + SEL_BLOCK) & (c_start[:, None] + CMP_LEN > j_start[None, :])).astype(jnp.float32)
    imp = jnp.einsum('bgrcn,nj->bgcj', p_cmp, cover)
    cur = pos_q // SEL_BLOCK
    jj = jnp.arange(n_sel)[None, :]
    causal = j_start[None, :] <= pos_q[:, None]
    forced = (jj == 0) | ((jj <= cur[:, None]) & (jj > cur[:, None] - N_LOCAL_BLOCKS))
    score = jnp.where(forced, FORCED_SCORE, jnp.where(causal, imp, -1.0))
    vals, idx = lax.top_k(score, min(SEL_TOPK, n_sel))
    blk_ok = vals >= 0.0
    pick = jax.vmap(jax.vmap(lambda blocks, ix: blocks[ix]))
    kg = pick(ks_blocks, idx)
    vg = pick(vs_blocks, idx)
    s2 = jnp.einsum('bcgrd,bgcksd->bgrcks', qg, kg).astype(jnp.float32) * scale
    kpos = idx[..., None] * SEL_BLOCK + jnp.arange(SEL_BLOCK)
    ok_s = blk_ok[..., None] & (kpos <= pos_q[None, None, :, None, None])
    s2 = jnp.where(ok_s[:, :, None], s2, -jnp.inf)
    n_keys = idx.shape[-1] * SEL_BLOCK
    p_sel = jax.nn.softmax(s2.reshape(B, N_KV_HEADS, Q_PER_KV, c, n_keys), axis=-1)
    p_sel = p_sel.reshape(s2.shape).astype(vg.dtype)
    o_sel = jnp.einsum('bgrcks,bgcksd->bcgrd', p_sel, vg)
    return (o_cmp.reshape(B, c, N_HEADS, HEAD_DIM), o_sel.reshape(B, c, N_HEADS, HEAD_DIM))


def nsa_window(q, pos0, k_ext, v_ext):
    B, S = q.shape[0], q.shape[1]
    P = k_ext.shape[1] - S
    c, nb = query_blocks(S)
    span = P + c
    idx = jnp.arange(nb)[:, None] * c + jnp.arange(span)[None, :]
    kb = k_ext[:, idx]
    vb = v_ext[:, idx]
    qb = q.reshape(B, nb, c, N_KV_HEADS, Q_PER_KV, HEAD_DIM)
    kp = (pos0 - P + idx)[:, None, :]
    qp = (pos0 + jnp.arange(S)).reshape(nb, c)[:, :, None]
    ok = (kp >= 0) & (kp <= qp) & (kp >= qp - WINDOW)
    s = jnp.einsum('bncgrd,bnsgd->bngrcs', qb, kb).astype(jnp.float32) * (HEAD_DIM ** -0.5)
    s = jnp.where(ok[None, :, None, None], s, -jnp.inf)
    p = jax.nn.softmax(s, axis=-1).astype(vb.dtype)
    o = jnp.einsum('bngrcs,bnsgd->bncgrd', p, vb)
    return o.reshape(B, S, N_HEADS, HEAD_DIM)


def nsa_mixer(q_raw, kv_raw, gate_raw, kv_past, win_past, pos0, cmp_pe, cmp_w1, cmp_w2):
    B, S = q_raw.shape[0], q_raw.shape[1]
    pos_q = pos0 + jnp.arange(S)
    q = rope(q_raw, pos_q)
    new_rows = jnp.stack([kv_raw[:, :, 0], kv_raw[:, :, 1], rope(kv_raw[:, :, 2], pos_q), kv_raw[:, :, 3]], axis=2)
    rows = jnp.concatenate([kv_past, new_rows], axis=1)
    T = rows.shape[1]
    kc = compress_blocks(rows[:, :, 0], cmp_pe[0], cmp_w1[0], cmp_w2[0])
    vc = compress_blocks(rows[:, :, 1], cmp_pe[1], cmp_w1[1], cmp_w2[1])
    pos_c = jnp.arange(kc.shape[1]) * CMP_STRIDE + (CMP_LEN - 1)
    kc = rope(kc, pos_c)
    n_sel = -(-T // SEL_BLOCK)
    pad = ((0, 0), (0, n_sel * SEL_BLOCK - T), (0, 0), (0, 0))

    def to_blocks(r):
        r = jnp.pad(r, pad).reshape(B, n_sel, SEL_BLOCK, N_KV_HEADS, HEAD_DIM)
        return jnp.transpose(r, (0, 3, 1, 2, 4))

    ks_blocks = to_blocks(rows[:, :, 2])
    vs_blocks = to_blocks(rows[:, :, 3])
    c, nb = query_blocks(S)
    q_blk = jnp.moveaxis(q.reshape(B, nb, c, N_HEADS, HEAD_DIM), 1, 0)
    pos_blk = pos_q.reshape(nb, c)
    o_cmp, o_sel = lax.map(lambda a: nsa_cmp_sel(a[0], a[1], kc, vc, pos_c, ks_blocks, vs_blocks), (q_blk, pos_blk))
    o_cmp = jnp.moveaxis(o_cmp, 0, 1).reshape(B, S, N_HEADS, HEAD_DIM)
    o_sel = jnp.moveaxis(o_sel, 0, 1).reshape(B, S, N_HEADS, HEAD_DIM)
    win_rows = jnp.stack([rope(kv_raw[:, :, 4], pos_q), kv_raw[:, :, 5]], axis=2)
    win_ext = jnp.concatenate([win_past, win_rows], axis=1)
    o_win = nsa_window(q, pos0, win_ext[:, :, 0], win_ext[:, :, 1])
    g = jax.nn.sigmoid(gate_raw.astype(jnp.float32)).reshape(B, S, 3, N_HEADS, 1).astype(q.dtype)
    o = g[:, :, 0] * o_cmp + g[:, :, 1] * o_sel + g[:, :, 2] * o_win
    win_new = win_ext[:, win_ext.shape[1] - min(WINDOW, pos0 + S):]
    return o.reshape(B, S, ATT_WIDTH), new_rows, win_new


def short_conv_mixer(b_gate, c_gate, x_in, prefix, conv_w):
    y, new_prefix = causal_dwconv(c_gate * x_in, prefix, conv_w)
    return b_gate * y, new_prefix


def conv_ffn(h, prefix, ffn_up, ffn_conv, ffn_down):
    up = h @ ffn_up
    a, b = up[..., :D_FF], up[..., D_FF:]
    a_conv, new_prefix = causal_dwconv(a, prefix, ffn_conv)
    return (jax.nn.silu(a_conv) * b) @ ffn_down, new_prefix


def layer(x, pos0, kv_past, win_past, pool_past, conv_past, ffn_past,
          g_mix, w_in, pool_w, pool_scale, cmp_pe, cmp_w1, cmp_w2, conv_w,
          w_br_pool, w_br_nsa, w_br_conv, w_out, g_ffn, ffn_up, ffn_conv, ffn_down):
    B, S = x.shape[0], x.shape[1]
    h = rmsnorm(x, g_mix)
    z = h @ w_in
    u_pool, q_raw, kv_raw, nsa_gate, c_b, c_c, c_x, gate_pool, gate_nsa, gate_conv = split_columns(z)
    y_pool, pool_new = pool_mixer(u_pool, pool_past, pos0, pool_w, pool_scale)
    y_nsa, kv_rows, win_new = nsa_mixer(q_raw.reshape(B, S, N_HEADS, HEAD_DIM),
                                        kv_raw.reshape(B, S, N_KV_STREAMS, N_KV_HEADS, HEAD_DIM),
                                        nsa_gate, kv_past, win_past, pos0, cmp_pe, cmp_w1, cmp_w2)
    y_conv, conv_new = short_conv_mixer(c_b, c_c, c_x, conv_past, conv_w)
    merged = (jax.nn.sigmoid(gate_pool) * (y_pool @ w_br_pool)
              + jax.nn.sigmoid(gate_nsa) * (y_nsa @ w_br_nsa)
              + jax.nn.sigmoid(gate_conv) * (y_conv @ w_br_conv))
    x = x + merged @ w_out
    f, ffn_new = conv_ffn(rmsnorm(x, g_ffn), ffn_past, ffn_up, ffn_conv, ffn_down)
    return x + f, kv_rows, win_new, pool_new, conv_new, ffn_new


def setup_inputs(seed: int = 0) -> dict:
    key = jax.random.key(seed)
    ks = jax.random.split(key, 28)
    n_pages = PAST_LEN // PAGE_SIZE
    n_used = DEC_BATCH * n_pages
    n_pool = n_used + n_used // 4
    w_s = min(WINDOW, PAST_LEN)

    def nrm(k, shape, s):
        return jax.random.normal(k, shape, jnp.float32) * s

    page_table = jax.random.permutation(ks[8], n_pool)[:n_used].reshape(DEC_BATCH, n_pages).astype(jnp.int32)
    return {
        'x_prompt': nrm(ks[0], (BATCH, SEQ, D_MODEL), 1.0),
        'x_sample': nrm(ks[1], (DEC_BATCH, DEC_SEQ, D_MODEL), 1.0),
        'cache_kv': nrm(ks[2], (DEPTH, n_pool, PAGE_SIZE, N_CACHED_STREAMS, N_KV_HEADS, HEAD_DIM), 1.0),
        'cache_win': nrm(ks[3], (DEPTH, DEC_BATCH, w_s, 2, N_KV_HEADS, HEAD_DIM), 1.0),
        'state_pool': nrm(ks[4], (DEPTH, DEC_BATCH, POOL_MAX - 1, POOL_WIDTH), 1.0),
        'state_conv': nrm(ks[5], (DEPTH, DEC_BATCH, CONV_WIDTH - 1, CONV_DIM), 1.0),
        'state_ffn': nrm(ks[6], (DEPTH, DEC_BATCH, CONV_WIDTH - 1, D_FF), 1.0),
        'page_table': page_table,
        'norm_mix': 1.0 + nrm(ks[9], (DEPTH, D_MODEL), 0.02),
        'w_in': nrm(ks[10], (DEPTH, D_MODEL, IN_WIDTH), D_MODEL ** -0.5),
        'pool_w': nrm(ks[11], (DEPTH, N_POOL_GROUPS, POOL_GROUP, POOL_GROUP), POOL_GROUP ** -0.5),
        'pool_scale': 1.0 + nrm(ks[12], (DEPTH, POOL_WIDTH), 0.02),
        'cmp_pe': nrm(ks[13], (DEPTH, 2, CMP_LEN, HEAD_DIM), 0.02),
        'cmp_w1': nrm(ks[14], (DEPTH, 2, CMP_LEN * HEAD_DIM, CMP_HIDDEN), (CMP_LEN * HEAD_DIM) ** -0.5),
        'cmp_w2': nrm(ks[15], (DEPTH, 2, CMP_HIDDEN, HEAD_DIM), CMP_HIDDEN ** -0.5),
        'conv_w': nrm(ks[16], (DEPTH, CONV_WIDTH, CONV_DIM), CONV_WIDTH ** -0.5),
        'w_br_pool': nrm(ks[17], (DEPTH, POOL_WIDTH, D_MODEL), POOL_WIDTH ** -0.5),
        'w_br_nsa': nrm(ks[18], (DEPTH, ATT_WIDTH, D_MODEL), ATT_WIDTH ** -0.5),
        'w_br_conv': nrm(ks[19], (DEPTH, CONV_DIM, D_MODEL), CONV_DIM ** -0.5),
        'w_out': nrm(ks[20], (DEPTH, D_MODEL, D_MODEL), D_MODEL ** -0.5),
        'norm_ffn': 1.0 + nrm(ks[21], (DEPTH, D_MODEL), 0.02),
        'ffn_up': nrm(ks[22], (DEPTH, D_MODEL, 2 * D_FF), D_MODEL ** -0.5),
        'ffn_conv': nrm(ks[23], (DEPTH, CONV_WIDTH, D_FF), CONV_WIDTH ** -0.5),
        'ffn_down': nrm(ks[24], (DEPTH, D_FF, D_MODEL), D_FF ** -0.5),
        'norm_final': 1.0 + nrm(ks[25], (D_MODEL,), 0.02),
    }


def reference(x_prompt, x_sample, cache_kv, cache_win, state_pool, state_conv, state_ffn, page_table,
              norm_mix, w_in, pool_w, pool_scale, cmp_pe, cmp_w1, cmp_w2, conv_w,
              w_br_pool, w_br_nsa, w_br_conv, w_out, norm_ffn, ffn_up, ffn_conv, ffn_down, norm_final):
    dt = x_prompt.dtype
    bp = x_prompt.shape[0]
    bs = x_sample.shape[0]
    n_pages = page_table.shape[1]
    kv0 = jnp.zeros((bp, 0, N_CACHED_STREAMS, N_KV_HEADS, HEAD_DIM), dt)
    win0 = jnp.zeros((bp, WINDOW, 2, N_KV_HEADS, HEAD_DIM), dt)
    pool0 = jnp.zeros((bp, POOL_MAX - 1, POOL_WIDTH), dt)
    conv0 = jnp.zeros((bp, CONV_WIDTH - 1, CONV_DIM), dt)
    ffn0 = jnp.zeros((bp, CONV_WIDTH - 1, D_FF), dt)
    kv_p, win_p, pool_p, conv_p, ffn_p = [], [], [], [], []
    kv_s, win_s, pool_s, conv_s, ffn_s = [], [], [], [], []
    xp, xs = x_prompt, x_sample
    for l in range(DEPTH):
        weights = (norm_mix[l], w_in[l], pool_w[l], pool_scale[l], cmp_pe[l], cmp_w1[l], cmp_w2[l], conv_w[l],
                   w_br_pool[l], w_br_nsa[l], w_br_conv[l], w_out[l], norm_ffn[l], ffn_up[l], ffn_conv[l], ffn_down[l])
        xp, a, b, c, d, e = layer(xp, 0, kv0, win0, pool0, conv0, ffn0, *weights)
        kv_p.append(a); win_p.append(b); pool_p.append(c); conv_p.append(d); ffn_p.append(e)
        kv_past = cache_kv[l][page_table].reshape(bs, n_pages * PAGE_SIZE, N_CACHED_STREAMS, N_KV_HEADS, HEAD_DIM)
        xs, a, b, c, d, e = layer(xs, PAST_LEN, kv_past, cache_win[l], state_pool[l], state_conv[l], state_ffn[l], *weights)
        kv_s.append(a); win_s.append(b); pool_s.append(c); conv_s.append(d); ffn_s.append(e)
    y_prompt = rmsnorm(xp, norm_final)
    y_sample = rmsnorm(xs, norm_final)
    kv_prompt = jnp.stack(kv_p)
    kv_sample = jnp.stack(kv_s)
    win_prompt = jnp.stack(win_p)
    win_sample = jnp.stack(win_s)
    pool_prompt = jnp.stack(pool_p)
    pool_sample = jnp.stack(pool_s)
    conv_prompt = jnp.stack(conv_p)
    conv_sample = jnp.stack(conv_s)
    ffn_prompt = jnp.stack(ffn_p)
    ffn_sample = jnp.stack(ffn_s)
    return (y_prompt, y_sample, kv_prompt, kv_sample, win_prompt, win_sample, pool_prompt, pool_sample, conv_prompt, conv_sample, ffn_prompt, ffn_sample)
```

```python
import functools

import jax
import jax.numpy as jnp
from jax import lax
from jax.experimental import pallas as pl
from jax.experimental.pallas import tpu as pltpu

D_MODEL = 1024
HEAD_DIM = 64
N_HEADS = 8
N_KV = 2
Q_PER_KV = 4
ATT_W = N_HEADS * HEAD_DIM
KV_W = N_KV * HEAD_DIM
CMP_LEN = 32
CMP_STRIDE = 16
CMP_HIDDEN = 128
SEL_BLOCK = 64
SEL_TOPK = 16
N_LOCAL = 2
WINDOW = 512
Q_BLOCK = 128
FORCED_SCORE = 1e4
POOL_WINDOWS = (2, 4, 8, 16)
POOL_W = 256
POOL_MAX = 16
CONV_TAPS = 3
CONV_DIM = 256
D_FF = 2816
ROPE_THETA = 10000.0
EPS = 1e-6
PAGE = 128
IN_SPLITS = (POOL_W, ATT_W, 6 * KV_W, 3 * N_HEADS, CONV_DIM, CONV_DIM, CONV_DIM, D_MODEL, D_MODEL, D_MODEL)

LANES = 128
SUBLANES = 8
V7X_VMEM_BYTES = 64 * 1024 * 1024
VMEM_CAP = 56 * 1024 * 1024

NEG = -1e30
MXU = jnp.bfloat16
F32 = jnp.float32

W_ROW = POOL_W + 5 * KV_W + 3 * KV_W + 3 * CONV_DIM + 3 * D_MODEL
W_T = 2 * ATT_W + 2 * KV_W + LANES
FF_CHUNK = 256


def _cparams(est_bytes):
    limit = int(min(max(est_bytes, 16 * 1024 * 1024), VMEM_CAP))
    return pltpu.CompilerParams(dimension_semantics=("arbitrary",), vmem_limit_bytes=limit)


def _cparams2(est_bytes):
    limit = int(min(max(est_bytes, 16 * 1024 * 1024), VMEM_CAP))
    return pltpu.CompilerParams(dimension_semantics=("arbitrary", "arbitrary"), vmem_limit_bytes=limit)


def _resident(shape):
    n = len(shape)
    return pl.BlockSpec(shape, lambda *_: (0,) * n, pipeline_mode=pl.Buffered(1))


def _sigmoid(x):
    return 1.0 / (1.0 + jnp.exp(-x))


def _dot(a, b):
    return jnp.dot(a, b, preferred_element_type=F32)


def _dot_nt(a, b):
    return lax.dot_general(a, b, (((1,), (1,)), ((), ())), preferred_element_type=F32)


def _split3(x):
    hi = x.astype(MXU)
    r1 = x - hi.astype(F32)
    mid = r1.astype(MXU)
    lo = (r1 - mid.astype(F32)).astype(MXU)
    return hi, mid, lo


def _inproj_kernel(x_ref, g_ref, wrow_ref, wt_ref, cs_ref, cst_ref,
                   up_ref, kv_ref, win_ref, cb_ref, ccx_ref, sg_ref, qt_ref, vt_ref, gt_ref, *, tm):
    x = x_ref[...]
    h = x * lax.rsqrt(jnp.mean(x * x, axis=-1, keepdims=True) + EPS)
    h = (h * g_ref[...]).astype(MXU)
    cos = cs_ref[:, 0:LANES]
    sin = cs_ref[:, LANES:2 * LANES]

    def mm(a, b):
        return _dot(h, wrow_ref[:, a:b])

    up_ref[...] = mm(0, 256)
    z = mm(256, 896)
    kv_ref[:, 0:256] = z[:, 0:256]
    kv_ref[:, 256:384] = z[:, 256:384] * cos + z[:, 384:512] * sin
    kv_ref[:, 384:512] = z[:, 512:640]
    z = mm(896, 1280)
    win_ref[:, 0:128] = z[:, 0:128] * cos + z[:, 128:256] * sin
    win_ref[:, 128:256] = z[:, 256:384]
    z = mm(1280, 2048)
    cb_ref[...] = z[:, 0:256]
    ccx_ref[...] = z[:, 256:512] * z[:, 512:768]
    for j in range(6):
        z = mm(2048 + 512 * j, 2048 + 512 * (j + 1))
        sg_ref[:, 512 * j:512 * (j + 1)] = _sigmoid(z).astype(sg_ref.dtype)

    zt = _dot_nt(wt_ref[...], h)
    cost = cst_ref[0:LANES, :]
    sint = cst_ref[LANES:2 * LANES, :]
    for c in range(tm // LANES):
        cols = slice(c * LANES, (c + 1) * LANES)
        for hp in range(ATT_W // LANES):
            rows = slice(hp * LANES, (hp + 1) * LANES)
            rrot = slice(ATT_W + hp * LANES, ATT_W + (hp + 1) * LANES)
            qt_ref[c, rows, :] = (zt[rows, cols] * cost[:, cols] + zt[rrot, cols] * sint[:, cols]).astype(qt_ref.dtype)
        vt_ref[c] = zt[2 * ATT_W:2 * ATT_W + 2 * KV_W, cols].astype(vt_ref.dtype)
        gt_ref[c] = _sigmoid(zt[2 * ATT_W + 2 * KV_W:W_T, cols])


def _in_proj(x, g, wrow, wt, cs, cst, *, tm):
    m = x.shape[0]
    nt = m // tm
    ntab = cs.shape[0] // tm
    nb = tm // LANES
    out_shape = (
        jax.ShapeDtypeStruct((m, POOL_W), F32),
        jax.ShapeDtypeStruct((m, 4 * KV_W), F32),
        jax.ShapeDtypeStruct((m, 2 * KV_W), F32),
        jax.ShapeDtypeStruct((m, CONV_DIM), F32),
        jax.ShapeDtypeStruct((m, CONV_DIM), F32),
        jax.ShapeDtypeStruct((m, 3 * D_MODEL), MXU),
        jax.ShapeDtypeStruct((m // LANES, ATT_W, LANES), MXU),
        jax.ShapeDtypeStruct((m // LANES, 2 * KV_W, LANES), MXU),
        jax.ShapeDtypeStruct((m // LANES, LANES, LANES), F32),
    )
    row = lambda w: pl.BlockSpec((tm, w), lambda i: (i, 0))
    blk = lambda r: pl.BlockSpec((nb, r, LANES), lambda i: (i, 0, 0))
    est = (2 * tm * D_MODEL * 4 + wrow.size * 2 + wt.size * 2 + 4 * tm * 256 * 4
           + 2 * tm * (256 + 512 + 256 + 256 + 256) * 4 + 2 * tm * 3072 * 2
           + 2 * tm * (ATT_W + 2 * KV_W) * 2 + 2 * tm * LANES * 4
           + tm * (W_T + 768 + 1024) * 4 + (4 << 20))
    return pl.pallas_call(
        functools.partial(_inproj_kernel, tm=tm),
        grid=(nt,),
        in_specs=[
            pl.BlockSpec((tm, D_MODEL), lambda i: (i, 0)),
            pl.BlockSpec((1, D_MODEL), lambda i: (0, 0)),
            _resident((D_MODEL, W_ROW)),
            _resident((W_T, D_MODEL)),
            pl.BlockSpec((tm, 2 * LANES), lambda i: (i % ntab, 0)),
            pl.BlockSpec((2 * LANES, tm), lambda i: (0, i % ntab)),
        ],
        out_specs=(row(POOL_W), row(4 * KV_W), row(2 * KV_W), row(CONV_DIM), row(CONV_DIM), row(3 * D_MODEL),
                   blk(ATT_W), blk(2 * KV_W), blk(LANES)),
        out_shape=out_shape,
        compiler_params=_cparams(est),
        name="in_proj",
    )(x, g, wrow, wt, cs, cst)


def _mix_kernel(up_ref, cb_ref, ccx_ref, ppre_ref, cpre_ref, pw_ref, ps_ref, cw_ref,
                yp_ref, yc_ref, pext, cext, *, tm, stride, tiles_per_seq, pos0):
    i = pl.program_id(0)
    hp = pext.shape[0] - tm
    hc = cext.shape[0] - tm

    @pl.when(i % tiles_per_seq == 0)
    def _():
        pext[0:hp, :] = ppre_ref[...]
        cext[0:hc, :] = cpre_ref[...]

    u = up_ref[...]
    pext[hp:hp + tm, :] = u
    acc = u
    sums = {}
    for k in range(1, POOL_MAX):
        acc = acc + pext[hp - k * stride:hp - k * stride + tm, :]
        if k + 1 in POOL_WINDOWS:
            sums[k + 1] = acc
    row = lax.broadcasted_iota(jnp.int32, (tm, 1), 0)
    if stride == 1:
        t_abs = pos0 + (i % tiles_per_seq) * tm + row
    else:
        t_abs = pos0 + lax.shift_right_logical(row, stride.bit_length() - 1)
    tp1 = (t_abs + 1).astype(F32)
    lane = lax.broadcasted_iota(jnp.int32, (1, POOL_W), 1)
    grp = POOL_W // len(POOL_WINDOWS)
    mean = None
    for gi, w in reversed(list(enumerate(POOL_WINDOWS))):
        mw = sums[w] / jnp.minimum(float(w), tp1)
        mean = mw if mean is None else jnp.where(lane < (gi + 1) * grp, mw, mean)
    pooled = mean - u
    y = _dot(pooled.astype(MXU), pw_ref[...]) * ps_ref[...]
    yp_ref[...] = y.astype(yp_ref.dtype)

    e0 = ccx_ref[...]
    cext[hc:hc + tm, :] = e0
    e1 = cext[hc - stride:hc - stride + tm, :]
    e2 = cext[hc - 2 * stride:hc - 2 * stride + tm, :]
    conv = cw_ref[0:1, :] * e2 + cw_ref[1:2, :] * e1 + cw_ref[2:3, :] * e0
    yc_ref[...] = (cb_ref[...] * conv).astype(yc_ref.dtype)

    if tiles_per_seq > 1:
        pext[0:hp, :] = pext[tm:tm + hp, :]
        cext[0:hc, :] = cext[tm:tm + hc, :]


def _mixers(up, cb, ccx, ppre, cpre, pw, ps, cw, *, tm, stride, tiles_per_seq, pos0):
    m = up.shape[0]
    nt = m // tm
    hp = ppre.shape[0] // (nt // tiles_per_seq)
    hc = cpre.shape[0] // (nt // tiles_per_seq)
    row = pl.BlockSpec((tm, 256), lambda i: (i, 0))
    est = 2 * 5 * tm * 256 * 4 + (2 * tm + hp + hc) * 256 * 4 + 2 * (hp + hc) * 256 * 4 + 12 * tm * 256 * 4 + (2 << 20)
    return pl.pallas_call(
        functools.partial(_mix_kernel, tm=tm, stride=stride, tiles_per_seq=tiles_per_seq, pos0=pos0),
        grid=(nt,),
        in_specs=[row, row, row,
                  pl.BlockSpec((hp, 256), lambda i: (i // tiles_per_seq, 0)),
                  pl.BlockSpec((hc, 256), lambda i: (i // tiles_per_seq, 0)),
                  pl.BlockSpec((256, 256), lambda i: (0, 0)),
                  pl.BlockSpec((1, 256), lambda i: (0, 0)),
                  pl.BlockSpec((CONV_TAPS, 256), lambda i: (0, 0))],
        out_specs=(row, row),
        out_shape=(jax.ShapeDtypeStruct((m, 256), MXU), jax.ShapeDtypeStruct((m, 256), MXU)),
        scratch_shapes=[pltpu.VMEM((hp + tm, 256), F32), pltpu.VMEM((hc + tm, 256), F32)],
        compiler_params=_cparams(est),
        name="mixers",
    )(up, cb, ccx, ppre, cpre, pw, ps, cw)


def _gelu_tanh(x):
    return x * (0.5 * (1.0 + jnp.tanh(0.7978845608028654 * (x + 0.044715 * (x * x * x)))))


def _compress_hidden(get_y, n, peab_ref, wab_ref, pb_scr):
    acts = []
    for s in range(2):
        y = get_y(s)
        la = (y + peab_ref[2 * s:2 * s + 1, :]).astype(MXU)
        lb = (y + peab_ref[2 * s + 1:2 * s + 2, :]).astype(MXU)
        pa = _dot(la, wab_ref[s, 0])
        pb = _dot(lb, wab_ref[s, 1])
        pb_scr[0:n, :] = pb
        pb_scr[n:n + SUBLANES, :] = jnp.zeros((SUBLANES, 2 * CMP_HIDDEN), F32)
        pre = pa + pb_scr[1:n + 1, :]
        acts.append(_gelu_tanh(pre).astype(MXU))
    return acts


def _cmp_prompt_kernel(x_ref, peab_ref, wab_ref, w2k_ref, w2vt_ref, csc_ref, kc_ref, vct_ref, pb_scr, *, n):
    def get_y(s):
        return jnp.concatenate(
            [x_ref[0, :, t * 512 + s * 128:t * 512 + (s + 1) * 128] for t in range(CMP_STRIDE)], axis=1)

    act_k, act_v = _compress_hidden(get_y, n, peab_ref, wab_ref, pb_scr)
    yk = _dot(act_k, w2k_ref[...])
    kc = yk[:, 0:LANES] * csc_ref[:, 0:LANES] + yk[:, LANES:2 * LANES] * csc_ref[:, LANES:2 * LANES]
    kc_ref[0] = kc.astype(kc_ref.dtype)
    vct_ref[0] = _dot_nt(w2vt_ref[...], act_v).astype(vct_ref.dtype)


def _compress_prompt(kv_chunks, peab, wab, w2k, w2vt, csc):
    b, n, _ = kv_chunks.shape
    est = 2 * n * 8192 * 4 + wab.size * 2 + 6 * n * 2048 * 4 + (4 << 20)
    return pl.pallas_call(
        functools.partial(_cmp_prompt_kernel, n=n),
        grid=(b,),
        in_specs=[pl.BlockSpec((1, n, 8192), lambda i: (i, 0, 0)),
                  pl.BlockSpec((SUBLANES, 2048), lambda i: (0, 0)),
                  _resident((2, 2, 2048, 256)),
                  pl.BlockSpec((256, 256), lambda i: (0, 0)),
                  pl.BlockSpec((128, 256), lambda i: (0, 0)),
                  pl.BlockSpec((n, 256), lambda i: (0, 0))],
        out_specs=(pl.BlockSpec((1, n, LANES), lambda i: (i, 0, 0)),
                   pl.BlockSpec((1, LANES, n), lambda i: (i, 0, 0))),
        out_shape=(jax.ShapeDtypeStruct((b, n, LANES), MXU), jax.ShapeDtypeStruct((b, LANES, n), MXU)),
        scratch_shapes=[pltpu.VMEM((n + SUBLANES, 2 * CMP_HIDDEN), F32)],
        compiler_params=_cparams(est),
        name="compress_prompt",
    )(kv_chunks, peab, wab, w2k, w2vt, csc)


def _attn_prompt_kernel(qt_ref, kc_ref, vct_ref, ksel_ref, kwin_ref, vt_ref, gt_ref, cov_ref,
                        y_ref, bias_scr, score_scr, *, n_cmp, n_sel):
    qb = pl.program_id(1)
    q0 = qb * Q_BLOCK
    nq = N_HEADS * Q_BLOCK
    half = Q_PER_KV * Q_BLOCK

    qt = qt_ref[0]
    zeros = jnp.zeros((HEAD_DIM, Q_BLOCK), qt.dtype)
    cols = []
    for hh in range(N_HEADS):
        piece = qt[hh * HEAD_DIM:(hh + 1) * HEAD_DIM, :]
        cols.append(jnp.concatenate([piece, zeros] if hh < Q_PER_KV else [zeros, piece], axis=0))
    qbd = jnp.concatenate(cols, axis=1)

    lane = lax.broadcasted_iota(jnp.int32, (1, nq), 1)
    qpos = q0 + jnp.bitwise_and(lane, Q_BLOCK - 1)

    sc = _dot(kc_ref[0], qbd)
    posc = lax.broadcasted_iota(jnp.int32, (n_cmp, 1), 0) * CMP_STRIDE + (CMP_LEN - 1)
    valid = posc <= qpos
    s = jnp.where(valid, sc, NEG)
    mx = jnp.max(s, axis=0, keepdims=True)
    mx = jnp.where(mx > 0.5 * NEG, mx, 0.0)
    e = jnp.where(valid, jnp.exp(s - mx), 0.0)
    p = e / jnp.maximum(jnp.sum(e, axis=0, keepdims=True), 1.0)
    pm = p.astype(MXU)
    vct = vct_ref[0]
    o_cmp = [_dot(vct[g * HEAD_DIM:(g + 1) * HEAD_DIM, :], pm[:, g * half:(g + 1) * half]) for g in range(N_KV)]

    jrow = lax.broadcasted_iota(jnp.int32, (n_sel, 1), 0)
    qp1 = q0 + lax.broadcasted_iota(jnp.int32, (1, Q_BLOCK), 1)
    cur = lax.shift_right_logical(qp1, 6)
    forced = (jrow == 0) | ((jrow <= cur) & (jrow > cur - N_LOCAL))
    causal = jrow * SEL_BLOCK <= qp1
    n_live = jnp.minimum(n_sel, 2 * qb + 2)
    for g in range(N_KV):
        ps = p[:, g * half:g * half + Q_BLOCK]
        for r in range(1, Q_PER_KV):
            ps = ps + p[:, g * half + r * Q_BLOCK:g * half + (r + 1) * Q_BLOCK]
        imp = None
        for term in _split3(ps):
            t = _dot(cov_ref[...], term)
            imp = t if imp is None else imp + t
        score = jnp.where(forced, FORCED_SCORE, jnp.where(causal, imp, -1.0))
        score_scr[...] = score

        def rank_body(jp, cnt, score=score):
            rowv = score_scr[pl.ds(jp, 1), :]
            ahead = (rowv > score) | ((rowv == score) & (jp < jrow))
            return cnt + jnp.where(ahead, 1.0, 0.0)

        cnt = lax.fori_loop(0, n_live, rank_body, jnp.zeros((n_sel, Q_BLOCK), F32))
        bias = jnp.where((cnt < float(SEL_TOPK)) & causal, 0.0, NEG)
        bias_scr[:, g * half:(g + 1) * half] = jnp.concatenate([bias] * Q_PER_KV, axis=1)

    def flash(n_chunks, chunk, key_of, v_row0, masker):
        def body(c, carry):
            m, l, a0, a1 = carry
            k0 = key_of(c)
            kk, kp = masker[0](k0)
            sij = _dot(kk, qbd)
            sij = masker[1](c, sij, kp)
            m_new = jnp.maximum(m, jnp.max(sij, axis=0, keepdims=True))
            alpha = jnp.exp(m - m_new)
            pij = jnp.exp(sij - m_new)
            l = alpha * l + jnp.sum(pij, axis=0, keepdims=True)
            pb = pij.astype(MXU)
            d0 = None
            d1 = None
            for sub in range(chunk // LANES):
                vblk = vt_ref[pl.ds(k0 // LANES + sub, 1), v_row0:v_row0 + KV_W, :][0]
                prow = pb[sub * LANES:(sub + 1) * LANES, :]
                t0 = _dot(vblk[0:HEAD_DIM, :], prow[:, 0:half])
                t1 = _dot(vblk[HEAD_DIM:KV_W, :], prow[:, half:nq])
                d0 = t0 if d0 is None else d0 + t0
                d1 = t1 if d1 is None else d1 + t1
            a0 = alpha[:, 0:half] * a0 + d0
            a1 = alpha[:, half:nq] * a1 + d1
            return m_new, l, a0, a1

        init = (jnp.full((1, nq), NEG, F32), jnp.zeros((1, nq), F32),
                jnp.zeros((HEAD_DIM, half), F32), jnp.zeros((HEAD_DIM, half), F32))
        return lax.fori_loop(n_chunks[0], n_chunks[1], body, init)

    sel_chunk = 2 * Q_BLOCK

    def sel_load(k0):
        kk = ksel_ref[0, pl.ds(k0, sel_chunk), :].astype(MXU)
        kp = k0 + lax.broadcasted_iota(jnp.int32, (sel_chunk, 1), 0)
        return kk, kp

    def sel_mask(c, sij, kp):
        rows = [jnp.broadcast_to(bias_scr[pl.ds(c * (sel_chunk // SEL_BLOCK) + jj, 1), :], (SEL_BLOCK, nq))
                for jj in range(sel_chunk // SEL_BLOCK)]
        return jnp.where(kp <= qpos, sij + jnp.concatenate(rows, axis=0), NEG)

    _, l_s, s0, s1 = flash((0, (qb + 2) // 2), sel_chunk, lambda c: pl.multiple_of(c * sel_chunk, sel_chunk),
                           0, (sel_load, sel_mask))

    def win_load(k0):
        kk = kwin_ref[0, pl.ds(k0, Q_BLOCK), :].astype(MXU)
        kp = k0 + lax.broadcasted_iota(jnp.int32, (Q_BLOCK, 1), 0)
        return kk, kp

    def win_mask(c, sij, kp):
        return jnp.where((kp <= qpos) & (kp >= qpos - WINDOW), sij, NEG)

    n_back = WINDOW // Q_BLOCK
    _, l_w, w0, w1 = flash((jnp.maximum(n_back - qb, 0), n_back + 1), Q_BLOCK,
                           lambda c: pl.multiple_of((qb - n_back + c) * Q_BLOCK, Q_BLOCK),
                           KV_W, (win_load, win_mask))

    inv_s = 1.0 / l_s
    inv_w = 1.0 / l_w
    sel_o = [s0 * inv_s[:, 0:half], s1 * inv_s[:, half:nq]]
    win_o = [w0 * inv_w[:, 0:half], w1 * inv_w[:, half:nq]]
    gt = gt_ref[0]
    outs = []
    for hh in range(N_HEADS):
        g, r = divmod(hh, Q_PER_KV)
        c = slice(r * Q_BLOCK, (r + 1) * Q_BLOCK)
        o = (o_cmp[g][:, c] * gt[hh:hh + 1, :] + sel_o[g][:, c] * gt[N_HEADS + hh:N_HEADS + hh + 1, :]
             + win_o[g][:, c] * gt[2 * N_HEADS + hh:2 * N_HEADS + hh + 1, :])
        outs.append(o)
    for pr in range(N_HEADS // 2):
        pair = jnp.concatenate([outs[2 * pr], outs[2 * pr + 1]], axis=0)
        y_ref[:, pr * LANES:(pr + 1) * LANES] = pair.T.astype(y_ref.dtype)


def _attention_prompt(qt, kc, vct, kv, win, vt, gt, cov, *, b, s):
    nqb = s // Q_BLOCK
    n_cmp = kc.shape[1]
    n_sel = s // SEL_BLOCK
    kv3 = kv.reshape(b, s, 4 * KV_W)
    win3 = win.reshape(b, s, 2 * KV_W)
    est = (2 * 2 * s * KV_W * 4 + 2 * (s // LANES) * 2 * KV_W * LANES * 2 + 12 * 256 * 1024 * 4
           + 4 * n_cmp * 1024 * 4 + (6 << 20))
    return pl.pallas_call(
        functools.partial(_attn_prompt_kernel, n_cmp=n_cmp, n_sel=n_sel),
        grid=(b, nqb),
        in_specs=[pl.BlockSpec((1, ATT_W, LANES), lambda i, j: (i * nqb + j, 0, 0)),
                  pl.BlockSpec((1, n_cmp, LANES), lambda i, j: (i, 0, 0)),
                  pl.BlockSpec((1, LANES, n_cmp), lambda i, j: (i, 0, 0)),
                  pl.BlockSpec((1, s, KV_W), lambda i, j: (i, 0, 2)),
                  pl.BlockSpec((1, s, KV_W), lambda i, j: (i, 0, 0)),
                  pl.BlockSpec((s // LANES, 2 * KV_W, LANES), lambda i, j: (i, 0, 0)),
                  pl.BlockSpec((1, LANES, LANES), lambda i, j: (i * nqb + j, 0, 0)),
                  pl.BlockSpec((n_sel, n_cmp), lambda i, j: (0, 0))],
        out_specs=pl.BlockSpec((Q_BLOCK, ATT_W), lambda i, j: (i * nqb + j, 0)),
        out_shape=jax.ShapeDtypeStruct((b * s, ATT_W), MXU),
        scratch_shapes=[pltpu.VMEM((n_sel, N_HEADS * Q_BLOCK), F32), pltpu.VMEM((n_sel, Q_BLOCK), F32)],
        compiler_params=_cparams2(est),
        name="attention_prompt",
    )(qt, kc, vct, kv3, win3, vt, gt, cov)


def _attn_sample_kernel(pt_ref, *refs, n_pages, n_new, pos0, n_sel, win_past):
    del pt_ref
    chunk_refs = refs[0:n_pages]
    row_refs = refs[n_pages:2 * n_pages]
    (q_ref, gate_ref, knew_ref, wnew_ref, wpast_ref, peab_ref, wab_ref, w2k_ref, w2v_ref, csc_ref,
     cov_ref, exp_ref, o_ref, pb_scr) = refs[2 * n_pages:]
    n = n_pages * (PAGE // CMP_STRIDE)
    rows = Q_PER_KV * n_new
    past = n_pages * PAGE

    def get_y(s):
        return jnp.concatenate(
            [jnp.concatenate([cr[0, :, t * 512 + s * 128:t * 512 + (s + 1) * 128] for t in range(CMP_STRIDE)], axis=1)
             for cr in chunk_refs], axis=0)

    act_k, act_v = _compress_hidden(get_y, n, peab_ref, wab_ref, pb_scr)
    yk = _dot(act_k, w2k_ref[...])
    kc = (yk[:, 0:LANES] * csc_ref[:, 0:LANES] + yk[:, LANES:2 * LANES] * csc_ref[:, LANES:2 * LANES]).astype(MXU)
    vc = _dot(act_v, w2v_ref[...]).astype(MXU)

    ksel = jnp.concatenate([rr[0, :, 0:KV_W] for rr in row_refs], axis=0).astype(MXU)
    vsel = jnp.concatenate([rr[0, :, KV_W:2 * KV_W] for rr in row_refs], axis=0).astype(MXU)
    kwin = wpast_ref[0, :, 0:KV_W].astype(MXU)
    vwin = wpast_ref[0, :, KV_W:2 * KV_W].astype(MXU)
    knew = knew_ref[0]
    wnew = wnew_ref[0]

    ri = lax.broadcasted_iota(jnp.int32, (rows, 1), 0)
    step = jnp.bitwise_and(ri, n_new - 1)
    qp = pos0 + step
    lane_c = lax.broadcasted_iota(jnp.int32, (1, n), 1)
    valid_c = lane_c * CMP_STRIDE + (CMP_LEN - 1) <= qp
    jl = lax.broadcasted_iota(jnp.int32, (1, LANES), 1)
    cur = lax.shift_right_logical(qp, 6)
    forced = (jl == 0) | ((jl <= cur) & (jl > cur - N_LOCAL))
    causal = (jl * SEL_BLOCK <= qp) & (jl < n_sel)
    lane_h = lax.broadcasted_iota(jnp.int32, (1, KV_W), 1)
    wl = lax.broadcasted_iota(jnp.int32, (1, win_past), 1)
    wpos = pos0 - win_past + wl
    valid_w = (wpos >= 0) & (wpos <= qp) & (wpos >= qp - WINDOW)

    def new_keys(qf, krows, vrows, ok_fn):
        out = []
        for tk in range(n_new):
            sv = jnp.sum(qf * krows[tk:tk + 1, :], axis=1, keepdims=True)
            out.append((jnp.where(ok_fn(tk), sv, NEG), vrows[tk:tk + 1, :]))
        return out

    def softmax_two(s_past, extra, v_past):
        mx = jnp.max(s_past, axis=1, keepdims=True)
        for sv, _ in extra:
            mx = jnp.maximum(mx, sv)
        pp = jnp.exp(s_past - mx)
        den = jnp.sum(pp, axis=1, keepdims=True)
        acc = _dot(pp.astype(MXU), v_past)
        for sv, vv in extra:
            pe = jnp.exp(sv - mx)
            den = den + pe
            acc = acc + pe * vv
        return acc / den

    total = jnp.zeros((rows, KV_W), F32)
    for g in range(N_KV):
        q = q_ref[0, g]
        qf = q.astype(F32)
        gate = gate_ref[0, g]
        head_lanes = (lane_h >= g * HEAD_DIM) & (lane_h < (g + 1) * HEAD_DIM)

        sc = _dot_nt(q, kc)
        s = jnp.where(valid_c, sc, NEG)
        mx = jnp.max(s, axis=1, keepdims=True)
        mx = jnp.where(mx > 0.5 * NEG, mx, 0.0)
        e = jnp.where(valid_c, jnp.exp(s - mx), 0.0)
        p = e / jnp.maximum(jnp.sum(e, axis=1, keepdims=True), 1.0)
        o_c = _dot(p.astype(MXU), vc)

        ps = p
        for r in range(1, Q_PER_KV):
            ps = ps + pltpu.roll(p, r * n_new, 0)
        imp = None
        for term in _split3(ps):
            t = _dot(term, cov_ref[...])
            imp = t if imp is None else imp + t
        score = jnp.where(forced, FORCED_SCORE, jnp.where(causal, imp, -1.0))
        cnt = jnp.zeros((rows, LANES), F32)
        for jp in range(n_sel):
            col = score[:, jp:jp + 1]
            cnt = cnt + jnp.where((col > score) | ((col == score) & (jp < jl)), 1.0, 0.0)
        sel = ((cnt < float(SEL_TOPK)) & causal).astype(F32)
        keymask = _dot(sel.astype(MXU), exp_ref[...])
        s_sel = jnp.where(keymask > 0.5, _dot_nt(q, ksel), NEG)
        new_blk = past // SEL_BLOCK
        sel_new = sel[:, new_blk:new_blk + 1] > 0.5
        extra = new_keys(qf, knew[:, 2 * KV_W:3 * KV_W], knew[:, 3 * KV_W:4 * KV_W],
                         lambda tk: sel_new & (tk <= step))
        o_s = softmax_two(s_sel, extra, vsel)

        s_win = jnp.where(valid_w, _dot_nt(q, kwin), NEG)
        extra = new_keys(qf, wnew[:, 0:KV_W], wnew[:, KV_W:2 * KV_W], lambda tk: tk <= step)
        o_w = softmax_two(s_win, extra, vwin)

        o = o_c * gate[:, 0:1] + o_s * gate[:, 1:2] + o_w * gate[:, 2:3]
        total = total + jnp.where(head_lanes, o, 0.0)
    o_ref[0] = total


def _attention_sample(page_table, cache_chunks, cache_rows, q, gate, knew, wnew, wpast, peab, wab, w2k, w2v,
                      csc, cov, expand, *, pos0):
    bd, n_pages = page_table.shape
    n = n_pages * (PAGE // CMP_STRIDE)
    n_new = q.shape[2] // Q_PER_KV
    past = n_pages * PAGE
    n_sel = -(-(past + n_new) // SEL_BLOCK)
    win_past = wpast.shape[1]

    def page_spec(shape, k, lane_blk):
        return pl.BlockSpec(shape, lambda i, pt, k=k: (pt[i, k], 0, lane_blk))

    in_specs = ([page_spec((1, PAGE // CMP_STRIDE, 8192), k, 0) for k in range(n_pages)]
                + [page_spec((1, PAGE, 2 * KV_W), k, 1) for k in range(n_pages)]
                + [pl.BlockSpec((1, N_KV, q.shape[2], KV_W), lambda i, pt: (i, 0, 0, 0)),
                   pl.BlockSpec((1, N_KV, q.shape[2], LANES), lambda i, pt: (i, 0, 0, 0)),
                   pl.BlockSpec((1, SUBLANES, 4 * KV_W), lambda i, pt: (i, 0, 0)),
                   pl.BlockSpec((1, SUBLANES, 2 * KV_W), lambda i, pt: (i, 0, 0)),
                   pl.BlockSpec((1, win_past, 2 * KV_W), lambda i, pt: (i, 0, 0)),
                   pl.BlockSpec((SUBLANES, 2048), lambda i, pt: (0, 0)),
                   pl.BlockSpec((2, 2, 2048, 256), lambda i, pt: (0, 0, 0, 0), pipeline_mode=pl.Buffered(1)),
                   pl.BlockSpec((256, 256), lambda i, pt: (0, 0)),
                   pl.BlockSpec((256, 128), lambda i, pt: (0, 0)),
                   pl.BlockSpec((n, 256), lambda i, pt: (0, 0)),
                   pl.BlockSpec((n, LANES), lambda i, pt: (0, 0)),
                   pl.BlockSpec((LANES, past), lambda i, pt: (0, 0))])
    est = (2 * n_pages * (8 * 8192 + PAGE * 2 * KV_W) * 4 + wab.size * 2 + 8 * n * 2048 * 4
           + 6 * past * KV_W * 4 + (6 << 20))
    return pl.pallas_call(
        functools.partial(_attn_sample_kernel, n_pages=n_pages, n_new=n_new, pos0=pos0, n_sel=n_sel,
                          win_past=win_past),
        grid_spec=pltpu.PrefetchScalarGridSpec(
            num_scalar_prefetch=1,
            grid=(bd,),
            in_specs=in_specs,
            out_specs=pl.BlockSpec((1, q.shape[2], KV_W), lambda i, pt: (i, 0, 0)),
            scratch_shapes=[pltpu.VMEM((n + SUBLANES, 2 * CMP_HIDDEN), F32)]),
        out_shape=jax.ShapeDtypeStruct((bd, q.shape[2], KV_W), F32),
        compiler_params=_cparams(est),
        name="attention_sample",
    )(page_table, *([cache_chunks] * n_pages), *([cache_rows] * n_pages), q, gate, knew, wnew, wpast,
      peab, wab, w2k, w2v, csc, cov, expand)


def _merge_kernel(x_ref, yp_ref, yn_ref, yc_ref, sg_ref, wbp_ref, wbn_ref, wbc_ref, wo_ref, o_ref):
    d = D_MODEL
    merged = (sg_ref[:, 0:d].astype(F32) * _dot(yp_ref[...], wbp_ref[...])
              + sg_ref[:, d:2 * d].astype(F32) * _dot(yn_ref[...], wbn_ref[...])
              + sg_ref[:, 2 * d:3 * d].astype(F32) * _dot(yc_ref[...], wbc_ref[...]))
    o_ref[...] = x_ref[...] + _dot(merged.astype(MXU), wo_ref[...])


def _merge(x, yp, yn, yc, sg, wbp, wbn, wbc, wo, *, tm):
    m = x.shape[0]
    row = lambda w: pl.BlockSpec((tm, w), lambda i: (i, 0))
    est = 2 * tm * (2 * D_MODEL * 4 + 1024 * 2 + 3072 * 2) + 2 * 3 * D_MODEL * D_MODEL + 6 * tm * D_MODEL * 4 + (2 << 20)
    return pl.pallas_call(
        _merge_kernel,
        grid=(m // tm,),
        in_specs=[row(D_MODEL), row(POOL_W), row(ATT_W), row(CONV_DIM), row(3 * D_MODEL),
                  _resident((POOL_W, D_MODEL)), _resident((ATT_W, D_MODEL)), _resident((CONV_DIM, D_MODEL)),
                  _resident((D_MODEL, D_MODEL))],
        out_specs=row(D_MODEL),
        out_shape=jax.ShapeDtypeStruct((m, D_MODEL), F32),
        compiler_params=_cparams(est),
        name="merge",
    )(x, yp, yn, yc, sg, wbp, wbn, wbc, wo)


def _ffn_kernel(x_ref, g_ref, pre_ref, wup_ref, wcv_ref, wdn_ref, gf_ref, o_ref, last_ref, *rest,
                tm, stride, tiles_per_seq, final):
    yf_ref = rest[0] if final else None
    aext = rest[-1]
    i = pl.program_id(0)
    hr = aext.shape[0] - tm

    @pl.when(i % tiles_per_seq == 0)
    def _():
        aext[0:hr, :] = pre_ref[...]

    x = x_ref[...]
    h = x * lax.rsqrt(jnp.mean(x * x, axis=-1, keepdims=True) + EPS)
    h = (h * g_ref[...]).astype(MXU)
    acc = x
    for j in range(D_FF // FF_CHUNK):
        cj = slice(j * FF_CHUNK, (j + 1) * FF_CHUNK)
        a = _dot(h, wup_ref[:, cj])
        aext[hr:hr + tm, cj] = a
        a1 = aext[hr - stride:hr - stride + tm, cj]
        a2 = aext[hr - 2 * stride:hr - 2 * stride + tm, cj]
        ac = wcv_ref[0:1, cj] * a2 + wcv_ref[1:2, cj] * a1 + wcv_ref[2:3, cj] * a
        bgate = _dot(h, wup_ref[:, D_FF + j * FF_CHUNK:D_FF + (j + 1) * FF_CHUNK])
        gated = (ac * _sigmoid(ac) * bgate).astype(MXU)
        acc = acc + _dot(gated, wdn_ref[cj, :])
    o_ref[...] = acc
    if final:
        yf = acc * lax.rsqrt(jnp.mean(acc * acc, axis=-1, keepdims=True) + EPS)
        yf_ref[...] = yf * gf_ref[...]
    keep = (CONV_TAPS - 1) * stride
    last_ref[0] = aext[hr + tm - keep:hr + tm, :]
    if tiles_per_seq > 1:
        aext[0:hr, :] = aext[tm:tm + hr, :]


def _conv_ffn(x, g, pre, wup, wcv, wdn, gf, *, tm, stride, tiles_per_seq, final):
    m = x.shape[0]
    nt = m // tm
    nseq = nt // tiles_per_seq
    hr = pre.shape[0] // nseq
    keep = (CONV_TAPS - 1) * stride
    row = pl.BlockSpec((tm, D_MODEL), lambda i: (i, 0))
    est = (3 * 2 * tm * D_MODEL * 4 + wup.size * 2 + wdn.size * 2 + (hr + tm) * D_FF * 4 + 2 * hr * D_FF * 4
           + 2 * keep * D_FF * 4 + 8 * tm * D_MODEL * 4 + (2 << 20))
    out_specs = [row, pl.BlockSpec((1, keep, D_FF), lambda i: (i // tiles_per_seq, 0, 0))]
    out_shape = [jax.ShapeDtypeStruct((m, D_MODEL), F32), jax.ShapeDtypeStruct((nseq, keep, D_FF), F32)]
    if final:
        out_specs.append(row)
        out_shape.append(jax.ShapeDtypeStruct((m, D_MODEL), F32))
    res = pl.pallas_call(
        functools.partial(_ffn_kernel, tm=tm, stride=stride, tiles_per_seq=tiles_per_seq, final=final),
        grid=(nt,),
        in_specs=[row, pl.BlockSpec((1, D_MODEL), lambda i: (0, 0)),
                  pl.BlockSpec((hr, D_FF), lambda i: (i // tiles_per_seq, 0)),
                  _resident((D_MODEL, 2 * D_FF)), pl.BlockSpec((CONV_TAPS, D_FF), lambda i: (0, 0)),
                  _resident((D_FF, D_MODEL)), pl.BlockSpec((1, D_MODEL), lambda i: (0, 0))],
        out_specs=tuple(out_specs),
        out_shape=tuple(out_shape),
        scratch_shapes=[pltpu.VMEM((hr + tm, D_FF), F32)],
        compiler_params=_cparams(est),
        name="conv_ffn",
    )(x, g, pre, wup, wcv, wdn, gf)
    return (res[0], res[1], res[2] if final else None)


def _rot_cols(w):
    k = w.shape[0]
    w4 = w.reshape(k, -1, 2, HEAD_DIM // 2)
    return jnp.stack([-w4[:, :, 1], w4[:, :, 0]], axis=2).reshape(k, -1)


def _rope_table(pos):
    half = HEAD_DIM // 2
    inv = ROPE_THETA ** (-jnp.arange(half, dtype=F32) / half)
    ang = pos.astype(F32)[:, None] * inv[None, :]
    return jnp.concatenate([jnp.tile(jnp.cos(ang), (1, 4)), jnp.tile(jnp.sin(ang), (1, 4))], axis=1)


def _blockdiag2(w):
    z = jnp.zeros_like(w)
    return jnp.concatenate([jnp.concatenate([w, z], axis=1), jnp.concatenate([z, w], axis=1)], axis=0)


def _layer_weights(w_in, pool_w, cmp_pe, cmp_w1, cmp_w2):
    parts = []
    off = 0
    for w in IN_SPLITS:
        parts.append(w_in[:, off:off + w])
        off += w
    u, q, kv, gate, cb, cc, cx, gp, gn, gc = parts
    kv6 = kv.reshape(D_MODEL, 6, KV_W)
    kcmp, vcmp, ksel, vsel, kwin, vwin = [kv6[:, s] for s in range(6)]
    wrow = jnp.concatenate([u, kcmp, vcmp, ksel, _rot_cols(ksel), vsel, kwin, _rot_cols(kwin), vwin,
                            cb, cc, cx, gp, gn, gc], axis=1).astype(MXU)
    qs = q * (HEAD_DIM ** -0.5)
    gpad = jnp.pad(gate, ((0, 0), (0, LANES - gate.shape[1])))
    wt = jnp.concatenate([qs, _rot_cols(qs), vsel, vwin, gpad], axis=1).T.astype(MXU)

    pw = jnp.zeros((POOL_W, POOL_W), F32)
    gsz = POOL_W // len(POOL_WINDOWS)
    for gi in range(len(POOL_WINDOWS)):
        pw = pw.at[gi * gsz:(gi + 1) * gsz, gi * gsz:(gi + 1) * gsz].set(pool_w[gi])

    wab = []
    peab = []
    for s in range(2):
        w1r = cmp_w1[s].reshape(CMP_LEN, HEAD_DIM, CMP_HIDDEN)
        halves = []
        for hsel in range(2):
            wh = w1r[hsel * CMP_STRIDE:(hsel + 1) * CMP_STRIDE]
            z = jnp.zeros((CMP_STRIDE, N_KV, HEAD_DIM, N_KV, CMP_HIDDEN), F32)
            for g in range(N_KV):
                z = z.at[:, g, :, g, :].set(wh)
            halves.append(z.reshape(CMP_STRIDE * KV_W, N_KV * CMP_HIDDEN))
            pe = cmp_pe[s][hsel * CMP_STRIDE:(hsel + 1) * CMP_STRIDE]
            peab.append(jnp.broadcast_to(pe[:, None, :], (CMP_STRIDE, N_KV, HEAD_DIM)).reshape(1, -1))
        wab.append(jnp.stack(halves))
    wab = jnp.stack(wab).astype(MXU)
    peab = jnp.concatenate(peab + [jnp.zeros((SUBLANES - 4, CMP_STRIDE * KV_W), F32)], axis=0)
    w2k = jnp.concatenate([_blockdiag2(cmp_w2[0]), _blockdiag2(_rot_cols(cmp_w2[0]))], axis=1).astype(MXU)
    w2v = _blockdiag2(cmp_w2[1]).astype(MXU)
    return dict(wrow=wrow, wt=wt, pw=pw.astype(MXU), wab=wab, peab=peab, w2k=w2k, w2v=w2v, w2vt=w2v.T)


def _cover(n_cmp, n_sel):
    c_start = jnp.arange(n_cmp) * CMP_STRIDE
    j_start = jnp.arange(n_sel) * SEL_BLOCK
    return ((c_start[:, None] < j_start[None, :] + SEL_BLOCK)
            & (c_start[:, None] + CMP_LEN > j_start[None, :])).astype(MXU)


def kernel(x_prompt, x_sample, cache_kv, cache_win, state_pool, state_conv, state_ffn, page_table,
           norm_mix, w_in, pool_w, pool_scale, cmp_pe, cmp_w1, cmp_w2, conv_w,
           w_br_pool, w_br_nsa, w_br_conv, w_out, norm_ffn, ffn_up, ffn_conv, ffn_down, norm_final):
    bp, sp, d = x_prompt.shape
    bd, sd, _ = x_sample.shape
    depth = w_in.shape[0]
    n_pages = page_table.shape[1]
    past = n_pages * PAGE
    n_pool = cache_kv.shape[1]
    wpast = cache_win.shape[2]
    assert d == D_MODEL and sp % (2 * Q_BLOCK) == 0 and sp >= WINDOW
    assert bd % SUBLANES == 0 and bd & (bd - 1) == 0 and sd & (sd - 1) == 0 and sd <= SUBLANES
    assert past % SEL_BLOCK == 0 and cache_kv.shape[2] == PAGE

    tm_p = 512
    mp = bp * sp
    md = bd * sd
    tps = sp // tm_p

    cs_p = _rope_table(jnp.arange(sp))
    cst_p = cs_p.T
    cs_d = _rope_table(jnp.repeat(past + jnp.arange(sd), bd))
    cst_d = cs_d.T
    nc_p = sp // CMP_STRIDE
    nc_d = past // CMP_STRIDE
    csc_p = _rope_table(jnp.arange(nc_p) * CMP_STRIDE + CMP_LEN - 1)
    csc_d = _rope_table(jnp.arange(nc_d) * CMP_STRIDE + CMP_LEN - 1)
    nsel_p = sp // SEL_BLOCK
    cov_pt = _cover(nc_p, nsel_p).T
    cov_d = _cover(nc_d, LANES)
    expand = (jnp.arange(LANES)[:, None] == (jnp.arange(past) // SEL_BLOCK)[None, :]).astype(MXU)

    hr_pool_p = 2 * SUBLANES
    hr_conv_p = SUBLANES
    zeros_pool = jnp.zeros((bp * hr_pool_p, POOL_W), F32)
    zeros_conv = jnp.zeros((bp * hr_conv_p, CONV_DIM), F32)
    zeros_ffn = jnp.zeros((bp * hr_conv_p, D_FF), F32)

    xp = x_prompt.reshape(mp, d)
    xs = jnp.transpose(x_sample, (1, 0, 2)).reshape(md, d)

    outs = {k: [] for k in ("kv_p", "kv_s", "win_p", "win_s", "pool_p", "pool_s", "conv_p", "conv_s", "ffn_p", "ffn_s")}
    yp_final = ys_final = None
    for l in range(depth):
        lw = _layer_weights(w_in[l], pool_w[l], cmp_pe[l], cmp_w1[l], cmp_w2[l])
        g_mix = norm_mix[l].reshape(1, d)
        g_ffn = norm_ffn[l].reshape(1, d)
        ps = pool_scale[l].reshape(1, POOL_W)
        cw = conv_w[l]
        wbp = w_br_pool[l].astype(MXU)
        wbn = w_br_nsa[l].astype(MXU)
        wbc = w_br_conv[l].astype(MXU)
        wo = w_out[l].astype(MXU)
        wup = ffn_up[l].astype(MXU)
        wdn = ffn_down[l].astype(MXU)
        wcv = ffn_conv[l]
        gf = norm_final.reshape(1, d)

        up, kv, win, cb, ccx, sg, qt, vt, gt = _in_proj(xp, g_mix, lw["wrow"], lw["wt"], cs_p, cst_p, tm=tm_p)
        ypool, yconv = _mixers(up, cb, ccx, zeros_pool, zeros_conv, lw["pw"], ps, cw,
                               tm=tm_p, stride=1, tiles_per_seq=tps, pos0=0)
        kc, vct = _compress_prompt(kv.reshape(bp, nc_p, CMP_STRIDE * 4 * KV_W), lw["peab"], lw["wab"], lw["w2k"],
                                   lw["w2vt"], csc_p)
        ynsa = _attention_prompt(qt, kc, vct, kv, win, vt, gt, cov_pt, b=bp, s=sp)
        x1 = _merge(xp, ypool, ynsa, yconv, sg, wbp, wbn, wbc, wo, tm=tm_p)
        xp, a_last, yp_final = _conv_ffn(x1, g_ffn, zeros_ffn, wup, wcv, wdn, gf, tm=tm_p, stride=1,
                                         tiles_per_seq=tps, final=l == depth - 1)
        outs["kv_p"].append(kv.reshape(bp, sp, 4, N_KV, HEAD_DIM))
        outs["win_p"].append(win.reshape(bp, sp, 2, N_KV, HEAD_DIM)[:, sp - WINDOW:])
        outs["pool_p"].append(up.reshape(bp, sp, POOL_W)[:, sp - (POOL_MAX - 1):])
        outs["conv_p"].append(ccx.reshape(bp, sp, CONV_DIM)[:, sp - (CONV_TAPS - 1):])
        outs["ffn_p"].append(a_last)

        up, kv, win, cb, ccx, sg, qt, vt, gt = _in_proj(xs, g_mix, lw["wrow"], lw["wt"], cs_d, cst_d, tm=md)
        pool_pre = jnp.transpose(state_pool[l], (1, 0, 2)).reshape((POOL_MAX - 1) * bd, POOL_W)
        conv_pre = jnp.transpose(state_conv[l], (1, 0, 2)).reshape((CONV_TAPS - 1) * bd, CONV_DIM)
        ffn_pre = jnp.transpose(state_ffn[l], (1, 0, 2)).reshape((CONV_TAPS - 1) * bd, D_FF)
        ypool, yconv = _mixers(up, cb, ccx, pool_pre, conv_pre, lw["pw"], ps, cw,
                               tm=md, stride=bd, tiles_per_seq=1, pos0=past)
        q_seq = jnp.transpose(qt.transpose(1, 0, 2).reshape(N_KV, Q_PER_KV, HEAD_DIM, sd, bd), (4, 0, 1, 3, 2))
        q_seq = q_seq.reshape(bd, N_KV, Q_PER_KV * sd, HEAD_DIM)
        zq = jnp.zeros_like(q_seq)
        q_pad = jnp.stack([jnp.concatenate([q_seq[:, 0], zq[:, 0]], axis=-1),
                           jnp.concatenate([zq[:, 1], q_seq[:, 1]], axis=-1)], axis=1)
        g_seq = gt.transpose(1, 0, 2).reshape(LANES, md)[0:3 * N_HEADS].reshape(3, N_KV, Q_PER_KV, sd, bd)
        g_seq = jnp.transpose(g_seq, (4, 1, 2, 3, 0)).reshape(bd, N_KV, Q_PER_KV * sd, 3)
        g_seq = jnp.pad(g_seq, ((0, 0), (0, 0), (0, 0), (0, LANES - 3)))
        kv_seq = jnp.transpose(kv.reshape(sd, bd, 4 * KV_W), (1, 0, 2))
        win_seq = jnp.transpose(win.reshape(sd, bd, 2 * KV_W), (1, 0, 2))
        knew = jnp.pad(kv_seq, ((0, 0), (0, SUBLANES - sd), (0, 0)))
        wnew = jnp.pad(win_seq, ((0, 0), (0, SUBLANES - sd), (0, 0)))
        ck = cache_kv[l]
        o_seq = _attention_sample(page_table, ck.reshape(n_pool, PAGE // CMP_STRIDE, CMP_STRIDE * 4 * KV_W),
                                  ck.reshape(n_pool, PAGE, 4 * KV_W), q_pad, g_seq, knew, wnew,
                                  cache_win[l].reshape(bd, wpast, 2 * KV_W), lw["peab"], lw["wab"], lw["w2k"],
                                  lw["w2v"], csc_d, cov_d, expand, pos0=past)
        ynsa = jnp.transpose(o_seq.reshape(bd, Q_PER_KV, sd, N_KV, HEAD_DIM), (2, 0, 3, 1, 4)).reshape(md, ATT_W)
        ynsa = ynsa.astype(MXU)
        x1 = _merge(xs, ypool, ynsa, yconv, sg, wbp, wbn, wbc, wo, tm=md)
        xs, a_last, ys_final = _conv_ffn(x1, g_ffn, ffn_pre, wup, wcv, wdn, gf, tm=md, stride=bd,
                                         tiles_per_seq=1, final=l == depth - 1)
        outs["kv_s"].append(kv_seq.reshape(bd, sd, 4, N_KV, HEAD_DIM))
        win_full = jnp.concatenate([cache_win[l], win_seq.reshape(bd, sd, 2, N_KV, HEAD_DIM)], axis=1)
        outs["win_s"].append(win_full[:, win_full.shape[1] - min(WINDOW, past + sd):])
        up_seq = jnp.transpose(up.reshape(sd, bd, POOL_W), (1, 0, 2))
        pool_full = jnp.concatenate([state_pool[l], up_seq], axis=1)
        outs["pool_s"].append(pool_full[:, pool_full.shape[1] - (POOL_MAX - 1):])
        ccx_seq = jnp.transpose(ccx.reshape(sd, bd, CONV_DIM), (1, 0, 2))
        conv_full = jnp.concatenate([state_conv[l], ccx_seq], axis=1)
        outs["conv_s"].append(conv_full[:, conv_full.shape[1] - (CONV_TAPS - 1):])
        outs["ffn_s"].append(jnp.transpose(a_last.reshape(CONV_TAPS - 1, bd, D_FF), (1, 0, 2)))

    y_prompt = yp_final.reshape(bp, sp, d)
    y_sample = jnp.transpose(ys_final.reshape(sd, bd, d), (1, 0, 2))
    st = lambda k: jnp.stack(outs[k])
    return (y_prompt, y_sample, st("kv_p"), st("kv_s"), st("win_p"), st("win_s"), st("pool_p"), st("pool_s"),
            st("conv_p"), st("conv_s"), st("ffn_p"), st("ffn_s"))
```

```python
import functools
import math

import jax
import jax.numpy as jnp
from jax import lax
from jax.experimental import pallas as pl
from jax.experimental.pallas import tpu as pltpu

D_MODEL = 1024
HEAD_DIM = 64
N_HEADS = 8
N_KV = 2
Q_PER_KV = 4
ATT_W = N_HEADS * HEAD_DIM
KV_W = N_KV * HEAD_DIM
CMP_LEN = 32
CMP_STRIDE = 16
CMP_HIDDEN = 128
SEL_BLOCK = 64
SEL_TOPK = 16
N_LOCAL = 2
WINDOW = 512
Q_BLOCK = 128
FORCED_SCORE = 1e4
POOL_WINDOWS = (2, 4, 8, 16)
POOL_W = 256
POOL_MAX = 16
CONV_TAPS = 3
CONV_DIM = 256
D_FF = 2816
ROPE_THETA = 10000.0
EPS = 1e-6
PAGE = 128
IN_SPLITS = (POOL_W, ATT_W, 6 * KV_W, 3 * N_HEADS, CONV_DIM, CONV_DIM, CONV_DIM, D_MODEL, D_MODEL, D_MODEL)

LANES = 128
SUBLANES = 8
BF16_ROWS = 16
V7X_VMEM_BYTES = 64 * 1024 * 1024
VMEM_CAP = 56 * 1024 * 1024

NEG = -1e30
MXU = jnp.bfloat16
F32 = jnp.float32
LOG2E = math.log2(math.e)

W_ROW = POOL_W + 3 * CONV_DIM + 3 * D_MODEL
T_Q = 2 * ATT_W
T_KV = 5 * KV_W
T_WIN = 3 * KV_W
W_T = T_Q + T_KV + T_WIN + LANES
VA_ROWS = HEAD_DIM + BF16_ROWS
Z_ALL = ATT_W + 4 * KV_W + 2 * KV_W + LANES
FF_CHUNK = 256


def _cparams(est_bytes, ndim=1):
    limit = int(min(max(est_bytes, 16 * 1024 * 1024), VMEM_CAP))
    return pltpu.CompilerParams(dimension_semantics=("arbitrary",) * ndim, vmem_limit_bytes=limit)


def _resident(shape):
    n = len(shape)
    return pl.BlockSpec(shape, lambda *_: (0,) * n, pipeline_mode=pl.Buffered(1))


def _sigmoid(x):
    return 1.0 / (1.0 + jnp.exp(-x))


def _dot(a, b):
    return jnp.dot(a, b, preferred_element_type=F32)


def _dot_nt(a, b):
    return lax.dot_general(a, b, (((1,), (1,)), ((), ())), preferred_element_type=F32)


def _split3(x):
    hi = x.astype(MXU)
    r1 = x - hi.astype(F32)
    mid = r1.astype(MXU)
    lo = (r1 - mid.astype(F32)).astype(MXU)
    return hi, mid, lo


def _dot3(x, w):
    out = None
    for term in _split3(x):
        t = _dot(term, w)
        out = t if out is None else out + t
    return out


def _inproj_kernel(x_ref, g_ref, wrow_ref, wt_ref, cst_ref, up_ref, cb_ref, ccx_ref, sg_ref, *outs, tm, sample):
    x = x_ref[...]
    h = x * lax.rsqrt(jnp.mean(x * x, axis=-1, keepdims=True) + EPS)
    h = (h * g_ref[...]).astype(MXU)

    up_ref[...] = _dot(h, wrow_ref[:, 0:POOL_W])
    z = _dot(h, wrow_ref[:, POOL_W:POOL_W + 3 * CONV_DIM])
    cb_ref[...] = z[:, 0:256]
    ccx_ref[...] = z[:, 256:512] * z[:, 512:768]
    g0 = POOL_W + 3 * CONV_DIM
    for j in range(6):
        z = _dot(h, wrow_ref[:, g0 + 512 * j:g0 + 512 * (j + 1)])
        sg_ref[:, 512 * j:512 * (j + 1)] = _sigmoid(z).astype(sg_ref.dtype)

    cost = cst_ref[0:LANES, :]
    sint = cst_ref[LANES:2 * LANES, :]
    zq = _dot_nt(wt_ref[0:T_Q, :], h)
    q = jnp.concatenate(
        [zq[i * LANES:(i + 1) * LANES] * cost + zq[ATT_W + i * LANES:ATT_W + (i + 1) * LANES] * sint
         for i in range(ATT_W // LANES)], axis=0)
    zk = _dot_nt(wt_ref[T_Q:T_Q + T_KV, :], h)
    ksel = zk[256:384] * cost + zk[384:512] * sint
    kv = jnp.concatenate([zk[0:256], ksel, zk[512:640]], axis=0)
    zw = _dot_nt(wt_ref[T_Q + T_KV:T_Q + T_KV + T_WIN, :], h)
    kwin = zw[0:128] * cost + zw[128:256] * sint
    win = jnp.concatenate([kwin, zw[256:384]], axis=0)
    gates = _sigmoid(_dot_nt(wt_ref[T_Q + T_KV + T_WIN:W_T, :], h))

    if sample:
        (zall_ref,) = outs
        for c in range(tm // LANES):
            cols = slice(c * LANES, (c + 1) * LANES)
            zall_ref[c] = jnp.concatenate([q[:, cols], kv[:, cols], win[:, cols], gates[:, cols]], axis=0)
    else:
        kvt_ref, wint_ref, qt_ref, va_ref, krows_ref, gt_ref = outs
        kvt_ref[0] = kv
        wint_ref[0] = win
        ones = jnp.ones((BF16_ROWS, LANES), va_ref.dtype)
        vsel = kv[384:512].astype(va_ref.dtype)
        vwin = win[128:256].astype(va_ref.dtype)
        for c in range(tm // LANES):
            cols = slice(c * LANES, (c + 1) * LANES)
            qt_ref[c] = q[:, cols].astype(qt_ref.dtype)
            va_ref[c] = jnp.concatenate(
                [vsel[0:64, cols], ones, vsel[64:128, cols], ones, vwin[0:64, cols], ones, vwin[64:128, cols], ones],
                axis=0)
            gt_ref[c] = gates[:, cols]
        krows_ref[:, 0:KV_W] = ksel.T.astype(krows_ref.dtype)
        krows_ref[:, KV_W:2 * KV_W] = kwin.T.astype(krows_ref.dtype)


def _in_proj(x, g, wrow, wt, cst, *, tm, seq, sample):
    m = x.shape[0]
    nt = m // tm
    ntab = cst.shape[1] // tm
    nb = tm // LANES
    tps = seq // tm if not sample else 1
    row = lambda w: pl.BlockSpec((tm, w), lambda i: (i, 0))
    blk = lambda r: pl.BlockSpec((nb, r, LANES), lambda i: (i, 0, 0))
    out_shape = [jax.ShapeDtypeStruct((m, POOL_W), F32), jax.ShapeDtypeStruct((m, CONV_DIM), F32),
                 jax.ShapeDtypeStruct((m, CONV_DIM), F32), jax.ShapeDtypeStruct((m, 3 * D_MODEL), MXU)]
    out_specs = [row(POOL_W), row(CONV_DIM), row(CONV_DIM), row(3 * D_MODEL)]
    if sample:
        out_shape += [jax.ShapeDtypeStruct((m // LANES, Z_ALL, LANES), F32)]
        out_specs += [blk(Z_ALL)]
    else:
        out_shape += [jax.ShapeDtypeStruct((m // seq, 4 * KV_W, seq), F32),
                      jax.ShapeDtypeStruct((m // seq, 2 * KV_W, seq), F32),
                      jax.ShapeDtypeStruct((m // LANES, ATT_W, LANES), MXU),
                      jax.ShapeDtypeStruct((m // LANES, 4 * VA_ROWS, LANES), MXU),
                      jax.ShapeDtypeStruct((m, 2 * KV_W), MXU),
                      jax.ShapeDtypeStruct((m // LANES, LANES, LANES), F32)]
        out_specs += [pl.BlockSpec((1, 4 * KV_W, tm), lambda i: (i // tps, 0, i % tps)),
                      pl.BlockSpec((1, 2 * KV_W, tm), lambda i: (i // tps, 0, i % tps)),
                      blk(ATT_W), blk(4 * VA_ROWS), row(2 * KV_W), blk(LANES)]
    est = (2 * tm * D_MODEL * 4 + wrow.size * 2 + wt.size * 2 + 4 * tm * 256 * 4
           + 2 * tm * (3 * 256 * 4 + 3072 * 2) + 2 * tm * Z_ALL * 4 + tm * (T_Q + T_KV + T_WIN + 1024 + 1024) * 4
           + (4 << 20))
    return pl.pallas_call(
        functools.partial(_inproj_kernel, tm=tm, sample=sample),
        grid=(nt,),
        in_specs=[pl.BlockSpec((tm, D_MODEL), lambda i: (i, 0)),
                  pl.BlockSpec((1, D_MODEL), lambda i: (0, 0)),
                  _resident((D_MODEL, W_ROW)),
                  _resident((W_T, D_MODEL)),
                  pl.BlockSpec((2 * LANES, tm), lambda i: (0, i % ntab))],
        out_specs=tuple(out_specs),
        out_shape=tuple(out_shape),
        compiler_params=_cparams(est),
        name="in_proj",
    )(x, g, wrow, wt, cst)


def _mix_kernel(up_ref, cb_ref, ccx_ref, ppre_ref, cpre_ref, pw_ref, ps_ref, cw_ref,
                yp_ref, yc_ref, pext, cext, *, tm, stride, tiles_per_seq, pos0):
    i = pl.program_id(0)
    hp = pext.shape[0] - tm
    hc = cext.shape[0] - tm

    @pl.when(i % tiles_per_seq == 0)
    def _():
        pext[0:hp, :] = ppre_ref[...]
        cext[0:hc, :] = cpre_ref[...]

    u = up_ref[...]
    pext[hp:hp + tm, :] = u
    acc = u
    sums = {}
    for k in range(1, POOL_MAX):
        acc = acc + pext[hp - k * stride:hp - k * stride + tm, :]
        if k + 1 in POOL_WINDOWS:
            sums[k + 1] = acc
    row = lax.broadcasted_iota(jnp.int32, (tm, 1), 0)
    if stride == 1:
        t_abs = pos0 + (i % tiles_per_seq) * tm + row
    else:
        t_abs = pos0 + lax.shift_right_logical(row, stride.bit_length() - 1)
    tp1 = (t_abs + 1).astype(F32)
    lane = lax.broadcasted_iota(jnp.int32, (1, POOL_W), 1)
    grp = POOL_W // len(POOL_WINDOWS)
    mean = None
    for gi, w in reversed(list(enumerate(POOL_WINDOWS))):
        mw = sums[w] / jnp.minimum(float(w), tp1)
        mean = mw if mean is None else jnp.where(lane < (gi + 1) * grp, mw, mean)
    pooled = mean - u
    y = _dot(pooled.astype(MXU), pw_ref[...]) * ps_ref[...]
    yp_ref[...] = y.astype(yp_ref.dtype)

    e0 = ccx_ref[...]
    cext[hc:hc + tm, :] = e0
    e1 = cext[hc - stride:hc - stride + tm, :]
    e2 = cext[hc - 2 * stride:hc - 2 * stride + tm, :]
    conv = cw_ref[0:1, :] * e2 + cw_ref[1:2, :] * e1 + cw_ref[2:3, :] * e0
    yc_ref[...] = (cb_ref[...] * conv).astype(yc_ref.dtype)

    if tiles_per_seq > 1:
        pext[0:hp, :] = pext[tm:tm + hp, :]
        cext[0:hc, :] = cext[tm:tm + hc, :]


def _mixers(up, cb, ccx, ppre, cpre, pw, ps, cw, *, tm, stride, tiles_per_seq, pos0):
    m = up.shape[0]
    nt = m // tm
    hp = ppre.shape[0] // (nt // tiles_per_seq)
    hc = cpre.shape[0] // (nt // tiles_per_seq)
    row = pl.BlockSpec((tm, 256), lambda i: (i, 0))
    est = 2 * 5 * tm * 256 * 4 + (2 * tm + hp + hc) * 256 * 4 + 2 * (hp + hc) * 256 * 4 + 12 * tm * 256 * 4 + (2 << 20)
    return pl.pallas_call(
        functools.partial(_mix_kernel, tm=tm, stride=stride, tiles_per_seq=tiles_per_seq, pos0=pos0),
        grid=(nt,),
        in_specs=[row, row, row,
                  pl.BlockSpec((hp, 256), lambda i: (i // tiles_per_seq, 0)),
                  pl.BlockSpec((hc, 256), lambda i: (i // tiles_per_seq, 0)),
                  pl.BlockSpec((256, 256), lambda i: (0, 0)),
                  pl.BlockSpec((1, 256), lambda i: (0, 0)),
                  pl.BlockSpec((CONV_TAPS, 256), lambda i: (0, 0))],
        out_specs=(row, row),
        out_shape=(jax.ShapeDtypeStruct((m, 256), MXU), jax.ShapeDtypeStruct((m, 256), MXU)),
        scratch_shapes=[pltpu.VMEM((hp + tm, 256), F32), pltpu.VMEM((hc + tm, 256), F32)],
        compiler_params=_cparams(est),
        name="mixers",
    )(up, cb, ccx, ppre, cpre, pw, ps, cw)


def _gelu_tanh(x):
    return x * (0.5 * (1.0 + jnp.tanh(0.7978845608028654 * (x + 0.044715 * (x * x * x)))))


def _compress_hidden(get_page, n_pages, perm_ref, peab_ref, wab_ref, pb_scr):
    n = n_pages * (PAGE // CMP_STRIDE)
    per = PAGE // CMP_STRIDE
    ys = ([], [])
    for p in range(n_pages):
        yp = _dot_nt(perm_ref[...], get_page(p).astype(MXU))
        for s in range(2):
            ys[s].append(jnp.concatenate(
                [yp[t * per:(t + 1) * per, s * KV_W:(s + 1) * KV_W] for t in range(CMP_STRIDE)], axis=1))
    pe3 = _split3(peab_ref[...])
    acts = []
    for s in range(2):
        rows = jnp.concatenate(ys[s], axis=0).astype(MXU)
        pa = _dot(rows, wab_ref[s, 0])
        pb = _dot(rows, wab_ref[s, 1])
        ba = None
        bb = None
        for term in pe3:
            ta = _dot(term, wab_ref[s, 0])
            tb = _dot(term, wab_ref[s, 1])
            ba = ta if ba is None else ba + ta
            bb = tb if bb is None else bb + tb
        bias = ba[2 * s:2 * s + 1, :] + bb[2 * s + 1:2 * s + 2, :]
        pb_scr[0:n, :] = pb
        pb_scr[n:n + SUBLANES, :] = jnp.zeros((SUBLANES, 2 * CMP_HIDDEN), F32)
        pre = pa + pb_scr[1:n + 1, :] + bias
        acts.append(_gelu_tanh(pre).astype(MXU))
    return acts


def _cmp_prompt_kernel(kvt_ref, perm_ref, peab_ref, wab_ref, w2k_ref, w2vt_ref, csc_ref, kc_ref, vct_ref, pb_scr,
                       *, n_pages):
    def get_page(p):
        return kvt_ref[0, :, p * PAGE:(p + 1) * PAGE]

    act_k, act_v = _compress_hidden(get_page, n_pages, perm_ref, peab_ref, wab_ref, pb_scr)
    yk = _dot(act_k, w2k_ref[...])
    kc = yk[:, 0:LANES] * csc_ref[:, 0:LANES] + yk[:, LANES:2 * LANES] * csc_ref[:, LANES:2 * LANES]
    kc_ref[0] = kc.astype(kc_ref.dtype)
    vct_ref[0] = _dot_nt(w2vt_ref[...], act_v).astype(vct_ref.dtype)


def _compress_prompt(kvt, perm, peab, wab, w2k, w2vt, csc):
    b, _, s = kvt.shape
    n_pages = s // PAGE
    n = s // CMP_STRIDE
    est = 2 * 256 * s * 4 + wab.size * 2 + 8 * n * 2048 * 4 + (4 << 20)
    return pl.pallas_call(
        functools.partial(_cmp_prompt_kernel, n_pages=n_pages),
        grid=(b,),
        in_specs=[pl.BlockSpec((1, 2 * KV_W, s), lambda i: (i, 0, 0)),
                  pl.BlockSpec((PAGE, PAGE), lambda i: (0, 0)),
                  pl.BlockSpec((SUBLANES, 2048), lambda i: (0, 0)),
                  _resident((2, 2, 2048, 256)),
                  pl.BlockSpec((256, 256), lambda i: (0, 0)),
                  pl.BlockSpec((128, 256), lambda i: (0, 0)),
                  pl.BlockSpec((n, 256), lambda i: (0, 0))],
        out_specs=(pl.BlockSpec((1, n, LANES), lambda i: (i, 0, 0)),
                   pl.BlockSpec((1, LANES, n), lambda i: (i, 0, 0))),
        out_shape=(jax.ShapeDtypeStruct((b, n, LANES), MXU), jax.ShapeDtypeStruct((b, LANES, n), MXU)),
        scratch_shapes=[pltpu.VMEM((n + SUBLANES, 2 * CMP_HIDDEN), F32)],
        compiler_params=_cparams(est),
        name="compress_prompt",
    )(kvt, perm, peab, wab, w2k, w2vt, csc)


def _attn_prompt_kernel(qt_ref, kc_ref, vct_ref, krows_ref, va_ref, gt_ref, cov_ref, oh_ref,
                        y_ref, qaug, score_scr, *, n_cmp, n_sel):
    qb = pl.program_id(1)
    q0 = qb * Q_BLOCK
    nq = N_HEADS * Q_BLOCK
    half = Q_PER_KV * Q_BLOCK

    qt = qt_ref[0]
    rsel = lax.broadcasted_iota(jnp.int32, (2 * HEAD_DIM, 1), 0) < HEAD_DIM
    zero = jnp.zeros((), qt.dtype)
    for g in range(N_KV):
        for r in range(Q_PER_KV):
            blk = qt[r * LANES:(r + 1) * LANES, :]
            keep = rsel if g == 0 else jnp.logical_not(rsel)
            qaug[0:LANES, (g * Q_PER_KV + r) * Q_BLOCK:(g * Q_PER_KV + r + 1) * Q_BLOCK] = jnp.where(keep, blk, zero)
    qbd = qaug[0:LANES, :]

    lane = lax.broadcasted_iota(jnp.int32, (1, nq), 1)
    qpos = q0 + jnp.bitwise_and(lane, Q_BLOCK - 1)

    sc = _dot(kc_ref[0], qbd)
    posc = lax.broadcasted_iota(jnp.int32, (n_cmp, 1), 0) * CMP_STRIDE + (CMP_LEN - 1)
    valid = posc <= qpos
    s = jnp.where(valid, sc, NEG)
    mx = jnp.max(s, axis=0, keepdims=True)
    mx = jnp.where(mx > 0.5 * NEG, mx, 0.0)
    e = jnp.where(valid, jnp.exp2(s - mx), 0.0)
    p = e / jnp.maximum(jnp.sum(e, axis=0, keepdims=True), 1.0)
    pm = p.astype(MXU)
    vct = vct_ref[0]
    o_cmp = [_dot(vct[g * HEAD_DIM:(g + 1) * HEAD_DIM, :], pm[:, g * half:(g + 1) * half]) for g in range(N_KV)]

    w2 = N_KV * Q_BLOCK
    jrow = lax.broadcasted_iota(jnp.int32, (n_sel, 1), 0)
    qp2 = q0 + jnp.bitwise_and(lax.broadcasted_iota(jnp.int32, (1, w2), 1), Q_BLOCK - 1)
    cur = lax.shift_right_logical(qp2, 6)
    forced = (jrow == 0) | ((jrow <= cur) & (jrow > cur - N_LOCAL))
    causal = jrow * SEL_BLOCK <= qp2
    psum = []
    for g in range(N_KV):
        ps = p[:, g * half:g * half + Q_BLOCK]
        for r in range(1, Q_PER_KV):
            ps = ps + p[:, g * half + r * Q_BLOCK:g * half + (r + 1) * Q_BLOCK]
        psum.append(ps)
    imp = None
    for term in _split3(jnp.concatenate(psum, axis=1)):
        t = _dot(cov_ref[...], term)
        imp = t if imp is None else imp + t
    score = jnp.where(forced, FORCED_SCORE, jnp.where(causal, imp, -1.0))
    score_scr[...] = score
    n_live = jnp.minimum(n_sel, 2 * qb + 2)

    def rank_body(jp, cnt):
        rowv = score_scr[pl.ds(jp, 1), :]
        ahead = (rowv > score) | ((rowv == score) & (jp < jrow))
        return cnt + jnp.where(ahead, 1.0, 0.0)

    cnt = lax.fori_loop(0, n_live, rank_body, jnp.zeros((n_sel, w2), F32))
    bias = jnp.where((cnt < float(SEL_TOPK)) & causal, 0.0, NEG).astype(MXU)
    qaug[LANES:2 * LANES, :] = jnp.zeros((LANES, nq), MXU)
    qaug[LANES:LANES + n_sel, :] = jnp.concatenate(
        [bias[:, 0:Q_BLOCK]] * Q_PER_KV + [bias[:, Q_BLOCK:w2]] * Q_PER_KV, axis=1)

    def step(carry, k0, kk, rhs, vrow0, nsub, mask):
        m, a0, a1 = carry
        sij = _dot(kk, rhs)
        if mask is not None:
            kp = k0 + lax.broadcasted_iota(jnp.int32, (kk.shape[0], 1), 0)
            sij = jnp.where(mask(kp), sij, NEG)
        m_new = jnp.maximum(m, jnp.max(sij, axis=0, keepdims=True))
        alpha = jnp.exp2(m - m_new)
        pij = jnp.exp2(sij - m_new).astype(MXU)
        d0 = None
        d1 = None
        for sub in range(nsub):
            vblk = va_ref[k0 // LANES + sub]
            prow = pij[sub * LANES:(sub + 1) * LANES, :]
            t0 = _dot(vblk[vrow0:vrow0 + VA_ROWS, :], prow[:, 0:half])
            t1 = _dot(vblk[vrow0 + VA_ROWS:vrow0 + 2 * VA_ROWS, :], prow[:, half:nq])
            d0 = t0 if d0 is None else d0 + t0
            d1 = t1 if d1 is None else d1 + t1
        return m_new, alpha[:, 0:half] * a0 + d0, alpha[:, half:nq] * a1 + d1

    def finish(carry):
        _, a0, a1 = carry
        return [a[0:HEAD_DIM] * (1.0 / a[HEAD_DIM:HEAD_DIM + 1]) for a in (a0, a1)]

    init = (jnp.full((1, nq), NEG, F32), jnp.zeros((VA_ROWS, half), F32), jnp.zeros((VA_ROWS, half), F32))

    sc_keys = 2 * Q_BLOCK

    def sel_step(c, carry, mask):
        k0 = pl.multiple_of(c * sc_keys, sc_keys)
        kk = jnp.concatenate([krows_ref[pl.ds(k0, sc_keys), 0:KV_W], oh_ref[pl.ds(k0, sc_keys), :]], axis=1)
        return step(carry, k0, kk, qaug[...], 0, sc_keys // LANES, mask)

    n_full = qb // 2
    carry = lax.fori_loop(0, n_full, lambda c, cr: sel_step(c, cr, None), init)
    sel_o = finish(sel_step(n_full, carry, lambda kp: kp <= qpos))

    n_back = WINDOW // Q_BLOCK

    def win_step(c, carry, mask):
        k0 = pl.multiple_of((qb - n_back + c) * Q_BLOCK, Q_BLOCK)
        kk = krows_ref[pl.ds(k0, Q_BLOCK), KV_W:2 * KV_W]
        return step(carry, k0, kk, qbd, 2 * VA_ROWS, 1, mask)

    lo = jnp.maximum(n_back - qb, 0)
    carry = lax.fori_loop(lo, 1, lambda c, cr: win_step(c, cr, lambda kp: kp >= qpos - WINDOW), init)
    carry = lax.fori_loop(jnp.maximum(lo, 1), n_back, lambda c, cr: win_step(c, cr, None), carry)
    win_o = finish(win_step(n_back, carry, lambda kp: kp <= qpos))

    gt = gt_ref[0]
    for r in range(Q_PER_KV):
        c = slice(r * Q_BLOCK, (r + 1) * Q_BLOCK)
        pair = []
        for g in range(N_KV):
            hh = g * Q_PER_KV + r
            pair.append(o_cmp[g][:, c] * gt[hh:hh + 1, :] + sel_o[g][:, c] * gt[N_HEADS + hh:N_HEADS + hh + 1, :]
                        + win_o[g][:, c] * gt[2 * N_HEADS + hh:2 * N_HEADS + hh + 1, :])
        y_ref[:, r * LANES:(r + 1) * LANES] = jnp.concatenate(pair, axis=0).T.astype(y_ref.dtype)


def _attention_prompt(qt, kc, vct, krows, va, gt, cov, onehot, *, b, s):
    nqb = s // Q_BLOCK
    n_cmp = kc.shape[1]
    n_sel = s // SEL_BLOCK
    est = (2 * s * 2 * KV_W * 2 + 2 * (s // LANES) * 4 * VA_ROWS * LANES * 2 + s * LANES * 2 + 10 * 256 * 1024 * 4
           + 5 * n_cmp * 1024 * 4 + (6 << 20))
    return pl.pallas_call(
        functools.partial(_attn_prompt_kernel, n_cmp=n_cmp, n_sel=n_sel),
        grid=(b, nqb),
        in_specs=[pl.BlockSpec((1, ATT_W, LANES), lambda i, j: (i * nqb + j, 0, 0)),
                  pl.BlockSpec((1, n_cmp, LANES), lambda i, j: (i, 0, 0)),
                  pl.BlockSpec((1, LANES, n_cmp), lambda i, j: (i, 0, 0)),
                  pl.BlockSpec((s, 2 * KV_W), lambda i, j: (i, 0)),
                  pl.BlockSpec((s // LANES, 4 * VA_ROWS, LANES), lambda i, j: (i, 0, 0)),
                  pl.BlockSpec((1, LANES, LANES), lambda i, j: (i * nqb + j, 0, 0)),
                  pl.BlockSpec((n_sel, n_cmp), lambda i, j: (0, 0)),
                  _resident((s, LANES))],
        out_specs=pl.BlockSpec((Q_BLOCK, ATT_W), lambda i, j: (i * nqb + j, 0)),
        out_shape=jax.ShapeDtypeStruct((b * s, ATT_W), MXU),
        scratch_shapes=[pltpu.VMEM((2 * LANES, N_HEADS * Q_BLOCK), MXU), pltpu.VMEM((n_sel, N_KV * Q_BLOCK), F32)],
        compiler_params=_cparams(est, 2),
        name="attention_prompt",
    )(qt, kc, vct, krows, va, gt, cov, onehot)


def _attn_sample_kernel(pt_ref, *refs, n_pages, n_new, pos0, n_sel, win_past):
    del pt_ref
    page_refs = refs[0:n_pages]
    (zrow_ref, wnewt_ref, wpast_ref, perm_ref, peab_ref, wab_ref, w2k_ref, w2v_ref, csc_ref,
     cov_ref, exp_ref, o_ref, wout_ref, pb_scr) = refs[n_pages:]
    n = n_pages * (PAGE // CMP_STRIDE)
    rows = Q_PER_KV * SUBLANES
    past = n_pages * PAGE

    act_k, act_v = _compress_hidden(lambda p: page_refs[p][0, 0, 0:2 * KV_W, :], n_pages, perm_ref, peab_ref, wab_ref,
                                    pb_scr)
    yk = _dot(act_k, w2k_ref[...])
    kc = (yk[:, 0:LANES] * csc_ref[:, 0:LANES] + yk[:, LANES:2 * LANES] * csc_ref[:, LANES:2 * LANES]).astype(MXU)
    vc = _dot(act_v, w2v_ref[...]).astype(MXU)

    kselt = jnp.concatenate([pr[0, 0, 2 * KV_W:3 * KV_W, :] for pr in page_refs], axis=1).astype(MXU)
    vselt = jnp.concatenate([pr[0, 0, 3 * KV_W:4 * KV_W, :] for pr in page_refs], axis=1).astype(MXU)
    wpast = wpast_ref[0, 0]
    kwint = wpast[0:KV_W].astype(MXU)
    vwint = wpast[KV_W:2 * KV_W].astype(MXU)
    zrow = zrow_ref[0]
    knew = zrow[:, ATT_W:ATT_W + 4 * KV_W]
    wnew = zrow[:, ATT_W + 4 * KV_W:ATT_W + 6 * KV_W]
    g_off = ATT_W + 6 * KV_W

    ri = lax.broadcasted_iota(jnp.int32, (rows, 1), 0)
    step = jnp.bitwise_and(ri, SUBLANES - 1)
    qp = pos0 + step
    lane_c = lax.broadcasted_iota(jnp.int32, (1, n), 1)
    valid_c = lane_c * CMP_STRIDE + (CMP_LEN - 1) <= qp
    jl = lax.broadcasted_iota(jnp.int32, (1, LANES), 1)
    cur = lax.shift_right_logical(qp, 6)
    forced = (jl == 0) | ((jl <= cur) & (jl > cur - N_LOCAL))
    causal = (jl * SEL_BLOCK <= qp) & (jl < n_sel)
    lane_h = lax.broadcasted_iota(jnp.int32, (1, KV_W), 1)
    wl = lax.broadcasted_iota(jnp.int32, (1, win_past), 1)
    wpos = pos0 - win_past + wl
    valid_w = (wpos >= 0) & (wpos <= qp) & (wpos >= qp - WINDOW)

    def new_keys(qf, krows, vrows, ok_fn):
        out = []
        for tk in range(n_new):
            sv = jnp.sum(qf * krows[tk:tk + 1, :], axis=1, keepdims=True)
            out.append((jnp.where(ok_fn(tk), sv, NEG), vrows[tk:tk + 1, :]))
        return out

    def softmax_two(s_past, extra, v_past_t):
        mx = jnp.max(s_past, axis=1, keepdims=True)
        for sv, _ in extra:
            mx = jnp.maximum(mx, sv)
        pp = jnp.exp2(s_past - mx)
        den = jnp.sum(pp, axis=1, keepdims=True)
        acc = _dot_nt(pp.astype(MXU), v_past_t)
        for sv, vv in extra:
            pe = jnp.exp2(sv - mx)
            den = den + pe
            acc = acc + pe * vv
        return acc / den

    total = jnp.zeros((rows, KV_W), F32)
    for g in range(N_KV):
        head_lanes = (lane_h >= g * HEAD_DIM) & (lane_h < (g + 1) * HEAD_DIM)
        qf = jnp.concatenate([jnp.where(head_lanes, zrow[:, r * LANES:(r + 1) * LANES], 0.0)
                              for r in range(Q_PER_KV)], axis=0)
        q = qf.astype(MXU)
        gate = [jnp.concatenate([zrow[:, g_off + k * N_HEADS + g * Q_PER_KV + r:
                                      g_off + k * N_HEADS + g * Q_PER_KV + r + 1] for r in range(Q_PER_KV)], axis=0)
                for k in range(3)]

        sc = _dot_nt(q, kc)
        s = jnp.where(valid_c, sc, NEG)
        mx = jnp.max(s, axis=1, keepdims=True)
        mx = jnp.where(mx > 0.5 * NEG, mx, 0.0)
        e = jnp.where(valid_c, jnp.exp2(s - mx), 0.0)
        p = e / jnp.maximum(jnp.sum(e, axis=1, keepdims=True), 1.0)
        o_c = _dot(p.astype(MXU), vc)

        ps = p
        for r in range(1, Q_PER_KV):
            ps = ps + pltpu.roll(p, r * SUBLANES, 0)
        imp = _dot3(ps, cov_ref[...])
        score = jnp.where(forced, FORCED_SCORE, jnp.where(causal, imp, -1.0))
        cnt = jnp.zeros((rows, LANES), F32)
        for jp in range(n_sel):
            col = score[:, jp:jp + 1]
            cnt = cnt + jnp.where((col > score) | ((col == score) & (jp < jl)), 1.0, 0.0)
        sel = ((cnt < float(SEL_TOPK)) & causal).astype(F32)
        keymask = _dot(sel.astype(MXU), exp_ref[...])
        s_sel = jnp.where(keymask > 0.5, _dot(q, kselt), NEG)
        new_blk = past // SEL_BLOCK
        sel_new = sel[:, new_blk:new_blk + 1] > 0.5
        extra = new_keys(qf, knew[:, 2 * KV_W:3 * KV_W], knew[:, 3 * KV_W:4 * KV_W],
                         lambda tk: sel_new & (tk <= step))
        o_s = softmax_two(s_sel, extra, vselt)

        s_win = jnp.where(valid_w, _dot(q, kwint), NEG)
        extra = new_keys(qf, wnew[:, 0:KV_W], wnew[:, KV_W:2 * KV_W], lambda tk: tk <= step)
        o_w = softmax_two(s_win, extra, vwint)

        o = o_c * gate[0] + o_s * gate[1] + o_w * gate[2]
        total = total + jnp.where(head_lanes, o, 0.0)
    o_ref[0] = total

    rolled = pltpu.roll(wpast, win_past - n_new, 1)
    newr = pltpu.roll(wnewt_ref[0], LANES - n_new, 1)
    wout_ref[0, :, 0:win_past - LANES] = rolled[:, 0:win_past - LANES]
    wout_ref[0, :, win_past - LANES:win_past] = jnp.where(jl >= LANES - n_new, newr,
                                                          rolled[:, win_past - LANES:win_past])


def _attention_sample(page_table, cache_t, zrow, wnewt, wpast_t, perm, peab, wab, w2k, w2v, csc, cov, expand,
                      *, layer, pos0, n_new):
    bd, n_pages = page_table.shape
    n = n_pages * (PAGE // CMP_STRIDE)
    past = n_pages * PAGE
    n_sel = -(-(past + n_new) // SEL_BLOCK)
    win_past = wpast_t.shape[3]
    rows = Q_PER_KV * SUBLANES

    in_specs = ([pl.BlockSpec((1, 1, 4 * KV_W, PAGE), lambda i, pt, k=k: (layer, pt[i, k], 0, 0))
                 for k in range(n_pages)]
                + [pl.BlockSpec((1, SUBLANES, Z_ALL), lambda i, pt: (i, 0, 0)),
                   pl.BlockSpec((1, 2 * KV_W, LANES), lambda i, pt: (i, 0, 0)),
                   pl.BlockSpec((1, 1, 2 * KV_W, win_past), lambda i, pt: (layer, i, 0, 0)),
                   pl.BlockSpec((PAGE, PAGE), lambda i, pt: (0, 0)),
                   pl.BlockSpec((SUBLANES, 2048), lambda i, pt: (0, 0)),
                   pl.BlockSpec((2, 2, 2048, 256), lambda i, pt: (0, 0, 0, 0), pipeline_mode=pl.Buffered(1)),
                   pl.BlockSpec((256, 256), lambda i, pt: (0, 0)),
                   pl.BlockSpec((256, 128), lambda i, pt: (0, 0)),
                   pl.BlockSpec((n, 256), lambda i, pt: (0, 0)),
                   pl.BlockSpec((n, LANES), lambda i, pt: (0, 0)),
                   pl.BlockSpec((LANES, past), lambda i, pt: (0, 0))])
    est = (2 * n_pages * 4 * KV_W * PAGE * 4 + wab.size * 2 + 8 * n * 2048 * 4 + 6 * past * KV_W * 4
           + 6 * 2 * KV_W * win_past * 4 + (6 << 20))
    return pl.pallas_call(
        functools.partial(_attn_sample_kernel, n_pages=n_pages, n_new=n_new, pos0=pos0, n_sel=n_sel,
                          win_past=win_past),
        grid_spec=pltpu.PrefetchScalarGridSpec(
            num_scalar_prefetch=1,
            grid=(bd,),
            in_specs=in_specs,
            out_specs=(pl.BlockSpec((1, rows, KV_W), lambda i, pt: (i, 0, 0)),
                       pl.BlockSpec((1, 2 * KV_W, win_past), lambda i, pt: (i, 0, 0))),
            scratch_shapes=[pltpu.VMEM((n + SUBLANES, 2 * CMP_HIDDEN), F32)]),
        out_shape=(jax.ShapeDtypeStruct((bd, rows, KV_W), F32),
                   jax.ShapeDtypeStruct((bd, 2 * KV_W, win_past), F32)),
        compiler_params=_cparams(est),
        name="attention_sample",
    )(page_table, *([cache_t] * n_pages), zrow, wnewt, wpast_t, perm, peab, wab, w2k, w2v, csc, cov, expand)


def _merge_kernel(x_ref, yp_ref, yn_ref, yc_ref, sg_ref, wbp_ref, wbn_ref, wbc_ref, wo_ref, o_ref):
    d = D_MODEL
    merged = (sg_ref[:, 0:d].astype(F32) * _dot(yp_ref[...], wbp_ref[...])
              + sg_ref[:, d:2 * d].astype(F32) * _dot(yn_ref[...], wbn_ref[...])
              + sg_ref[:, 2 * d:3 * d].astype(F32) * _dot(yc_ref[...], wbc_ref[...]))
    o_ref[...] = x_ref[...] + _dot(merged.astype(MXU), wo_ref[...])


def _merge(x, yp, yn, yc, sg, wbp, wbn, wbc, wo, *, tm):
    m = x.shape[0]
    row = lambda w: pl.BlockSpec((tm, w), lambda i: (i, 0))
    est = 2 * tm * (2 * D_MODEL * 4 + 1024 * 2 + 3072 * 2) + 2 * 3 * D_MODEL * D_MODEL + 6 * tm * D_MODEL * 4 + (2 << 20)
    return pl.pallas_call(
        _merge_kernel,
        grid=(m // tm,),
        in_specs=[row(D_MODEL), row(POOL_W), row(ATT_W), row(CONV_DIM), row(3 * D_MODEL),
                  _resident((POOL_W, D_MODEL)), _resident((ATT_W, D_MODEL)), _resident((CONV_DIM, D_MODEL)),
                  _resident((D_MODEL, D_MODEL))],
        out_specs=row(D_MODEL),
        out_shape=jax.ShapeDtypeStruct((m, D_MODEL), F32),
        compiler_params=_cparams(est),
        name="merge",
    )(x, yp, yn, yc, sg, wbp, wbn, wbc, wo)


def _ffn_kernel(x_ref, g_ref, pre_ref, wup_ref, wcv_ref, wdn_ref, gf_ref, o_ref, last_ref, *rest,
                tm, stride, tiles_per_seq, final):
    yf_ref = rest[0] if final else None
    aext = rest[-1]
    i = pl.program_id(0)
    hr = aext.shape[0] - tm

    @pl.when(i % tiles_per_seq == 0)
    def _():
        aext[0:hr, :] = pre_ref[...]

    x = x_ref[...]
    h = x * lax.rsqrt(jnp.mean(x * x, axis=-1, keepdims=True) + EPS)
    h = (h * g_ref[...]).astype(MXU)
    acc = x
    for j in range(D_FF // FF_CHUNK):
        cj = slice(j * FF_CHUNK, (j + 1) * FF_CHUNK)
        a = _dot(h, wup_ref[:, cj])
        aext[hr:hr + tm, cj] = a
        a1 = aext[hr - stride:hr - stride + tm, cj]
        a2 = aext[hr - 2 * stride:hr - 2 * stride + tm, cj]
        ac = wcv_ref[0:1, cj] * a2 + wcv_ref[1:2, cj] * a1 + wcv_ref[2:3, cj] * a
        bgate = _dot(h, wup_ref[:, D_FF + j * FF_CHUNK:D_FF + (j + 1) * FF_CHUNK])
        gated = (ac * _sigmoid(ac) * bgate).astype(MXU)
        acc = acc + _dot(gated, wdn_ref[cj, :])
    o_ref[...] = acc
    if final:
        yf = acc * lax.rsqrt(jnp.mean(acc * acc, axis=-1, keepdims=True) + EPS)
        yf_ref[...] = yf * gf_ref[...]
    keep = (CONV_TAPS - 1) * stride
    last_ref[0] = aext[hr + tm - keep:hr + tm, :]
    if tiles_per_seq > 1:
        aext[0:hr, :] = aext[tm:tm + hr, :]


def _conv_ffn(x, g, pre, wup, wcv, wdn, gf, *, tm, stride, tiles_per_seq, final):
    m = x.shape[0]
    nt = m // tm
    nseq = nt // tiles_per_seq
    hr = pre.shape[0] // nseq
    keep = (CONV_TAPS - 1) * stride
    row = pl.BlockSpec((tm, D_MODEL), lambda i: (i, 0))
    est = (3 * 2 * tm * D_MODEL * 4 + wup.size * 2 + wdn.size * 2 + (hr + tm) * D_FF * 4 + 2 * hr * D_FF * 4
           + 2 * keep * D_FF * 4 + 8 * tm * D_MODEL * 4 + (2 << 20))
    out_specs = [row, pl.BlockSpec((1, keep, D_FF), lambda i: (i // tiles_per_seq, 0, 0))]
    out_shape = [jax.ShapeDtypeStruct((m, D_MODEL), F32), jax.ShapeDtypeStruct((nseq, keep, D_FF), F32)]
    if final:
        out_specs.append(row)
        out_shape.append(jax.ShapeDtypeStruct((m, D_MODEL), F32))
    res = pl.pallas_call(
        functools.partial(_ffn_kernel, tm=tm, stride=stride, tiles_per_seq=tiles_per_seq, final=final),
        grid=(nt,),
        in_specs=[row, pl.BlockSpec((1, D_MODEL), lambda i: (0, 0)),
                  pl.BlockSpec((hr, D_FF), lambda i: (i // tiles_per_seq, 0)),
                  _resident((D_MODEL, 2 * D_FF)), pl.BlockSpec((CONV_TAPS, D_FF), lambda i: (0, 0)),
                  _resident((D_FF, D_MODEL)), pl.BlockSpec((1, D_MODEL), lambda i: (0, 0))],
        out_specs=tuple(out_specs),
        out_shape=tuple(out_shape),
        scratch_shapes=[pltpu.VMEM((hr + tm, D_FF), F32)],
        compiler_params=_cparams(est),
        name="conv_ffn",
    )(x, g, pre, wup, wcv, wdn, gf)
    return (res[0], res[1], res[2] if final else None)


def _rot_cols(w):
    k = w.shape[0]
    w4 = w.reshape(k, -1, 2, HEAD_DIM // 2)
    return jnp.stack([-w4[:, :, 1], w4[:, :, 0]], axis=2).reshape(k, -1)


def _rope_table_t(pos):
    half = HEAD_DIM // 2
    inv = ROPE_THETA ** (-jnp.arange(half, dtype=F32) / half)
    ang = pos.astype(F32)[:, None] * inv[None, :]
    return jnp.concatenate([jnp.tile(jnp.cos(ang), (1, 4)), jnp.tile(jnp.sin(ang), (1, 4))], axis=1).T


def _blockdiag2(w):
    z = jnp.zeros_like(w)
    return jnp.concatenate([jnp.concatenate([w, z], axis=1), jnp.concatenate([z, w], axis=1)], axis=0)


def _head_major_to_pair_major(w, axis):
    shp = w.shape
    w = w.reshape(shp[:axis] + (N_KV, Q_PER_KV, HEAD_DIM) + shp[axis + 1:])
    w = jnp.swapaxes(w, axis, axis + 1)
    return w.reshape(shp)


def _layer_weights(w_in, pool_w, cmp_pe, cmp_w1, cmp_w2, w_br_nsa):
    parts = []
    off = 0
    for w in IN_SPLITS:
        parts.append(w_in[:, off:off + w])
        off += w
    u, q, kv, gate, cb, cc, cx, gp, gn, gc = parts
    kv6 = kv.reshape(D_MODEL, 6, KV_W)
    kcmp, vcmp, ksel, vsel, kwin, vwin = [kv6[:, s] for s in range(6)]
    wrow = jnp.concatenate([u, cb, cc, cx, gp, gn, gc], axis=1).astype(MXU)
    qs = _head_major_to_pair_major(q, 1) * (HEAD_DIM ** -0.5 * LOG2E)
    gpad = jnp.pad(gate, ((0, 0), (0, LANES - gate.shape[1])))
    wt = jnp.concatenate([qs, _rot_cols(qs), kcmp, vcmp, ksel, _rot_cols(ksel), vsel,
                          kwin, _rot_cols(kwin), vwin, gpad], axis=1).T.astype(MXU)
    wbn = _head_major_to_pair_major(w_br_nsa, 0).astype(MXU)

    pw = jnp.zeros((POOL_W, POOL_W), F32)
    gsz = POOL_W // len(POOL_WINDOWS)
    for gi in range(len(POOL_WINDOWS)):
        pw = pw.at[gi * gsz:(gi + 1) * gsz, gi * gsz:(gi + 1) * gsz].set(pool_w[gi])

    wab = []
    peab = []
    for s in range(2):
        w1r = cmp_w1[s].reshape(CMP_LEN, HEAD_DIM, CMP_HIDDEN)
        halves = []
        for hsel in range(2):
            wh = w1r[hsel * CMP_STRIDE:(hsel + 1) * CMP_STRIDE]
            z = jnp.zeros((CMP_STRIDE, N_KV, HEAD_DIM, N_KV, CMP_HIDDEN), F32)
            for g in range(N_KV):
                z = z.at[:, g, :, g, :].set(wh)
            halves.append(z.reshape(CMP_STRIDE * KV_W, N_KV * CMP_HIDDEN))
            pe = cmp_pe[s][hsel * CMP_STRIDE:(hsel + 1) * CMP_STRIDE]
            peab.append(jnp.broadcast_to(pe[:, None, :], (CMP_STRIDE, N_KV, HEAD_DIM)).reshape(1, -1))
        wab.append(jnp.stack(halves))
    wab = jnp.stack(wab).astype(MXU)
    peab = jnp.concatenate(peab + [jnp.zeros((SUBLANES - 4, CMP_STRIDE * KV_W), F32)], axis=0)
    w2k = jnp.concatenate([_blockdiag2(cmp_w2[0]), _blockdiag2(_rot_cols(cmp_w2[0]))], axis=1).astype(MXU)
    w2v = _blockdiag2(cmp_w2[1]).astype(MXU)
    return dict(wrow=wrow, wt=wt, wbn=wbn, pw=pw.astype(MXU), wab=wab, peab=peab, w2k=w2k, w2v=w2v, w2vt=w2v.T)


def _cover(n_cmp, n_sel):
    c_start = jnp.arange(n_cmp) * CMP_STRIDE
    j_start = jnp.arange(n_sel) * SEL_BLOCK
    return ((c_start[:, None] < j_start[None, :] + SEL_BLOCK)
            & (c_start[:, None] + CMP_LEN > j_start[None, :])).astype(MXU)


def _feature_major(x):
    nd = x.ndim
    perm = tuple(range(nd - 4)) + (nd - 3, nd - 2, nd - 1, nd - 4)
    xt = jnp.transpose(x, perm)
    return xt.reshape(xt.shape[:nd - 4] + (-1, xt.shape[-1]))


def _token_major(xt, a, b, c):
    nd = xt.ndim
    x = xt.reshape(xt.shape[:nd - 2] + (a, b, c, xt.shape[-1]))
    perm = tuple(range(nd - 2)) + (nd + 1, nd - 2, nd - 1, nd)
    return jnp.transpose(x, perm)


def kernel(x_prompt, x_sample, cache_kv, cache_win, state_pool, state_conv, state_ffn, page_table,
           norm_mix, w_in, pool_w, pool_scale, cmp_pe, cmp_w1, cmp_w2, conv_w,
           w_br_pool, w_br_nsa, w_br_conv, w_out, norm_ffn, ffn_up, ffn_conv, ffn_down, norm_final):
    bp, sp, d = x_prompt.shape
    bd, sd, _ = x_sample.shape
    depth = w_in.shape[0]
    n_pages = page_table.shape[1]
    past = n_pages * PAGE
    wpast = cache_win.shape[2]
    assert d == D_MODEL and sp % (2 * Q_BLOCK) == 0 and sp >= WINDOW and sp // SEL_BLOCK <= LANES
    assert bd == LANES and sd & (sd - 1) == 0 and sd <= SUBLANES
    assert past % SEL_BLOCK == 0 and cache_kv.shape[2] == PAGE and wpast == WINDOW

    tm_p = 512
    mp = bp * sp
    md = bd * sd
    tps = sp // tm_p

    cst_p = _rope_table_t(jnp.arange(sp))
    cst_d = _rope_table_t(jnp.repeat(past + jnp.arange(sd), bd))
    nc_p = sp // CMP_STRIDE
    nc_d = past // CMP_STRIDE
    csc_p = _rope_table_t(jnp.arange(nc_p) * CMP_STRIDE + CMP_LEN - 1).T
    csc_d = _rope_table_t(jnp.arange(nc_d) * CMP_STRIDE + CMP_LEN - 1).T
    nsel_p = sp // SEL_BLOCK
    cov_pt = _cover(nc_p, nsel_p).T
    cov_d = _cover(nc_d, LANES)
    expand = (jnp.arange(LANES)[:, None] == (jnp.arange(past) // SEL_BLOCK)[None, :]).astype(MXU)
    onehot_p = ((jnp.arange(sp) // SEL_BLOCK)[:, None] == jnp.arange(LANES)[None, :]).astype(MXU)
    tok = jnp.arange(PAGE)
    perm = (tok[None, :] == (tok[:, None] % (PAGE // CMP_STRIDE)) * CMP_STRIDE + tok[:, None] // (PAGE // CMP_STRIDE))
    perm = perm.astype(MXU)

    hr_pool_p = 2 * SUBLANES
    hr_conv_p = SUBLANES
    zeros_pool = jnp.zeros((bp * hr_pool_p, POOL_W), F32)
    zeros_conv = jnp.zeros((bp * hr_conv_p, CONV_DIM), F32)
    zeros_ffn = jnp.zeros((bp * hr_conv_p, D_FF), F32)

    cache_t = _feature_major(cache_kv)
    win_t = _feature_major(cache_win)
    pool_tm = jnp.transpose(state_pool, (0, 2, 1, 3))

    xp = x_prompt.reshape(mp, d)
    xs = jnp.transpose(x_sample, (1, 0, 2)).reshape(md, d)

    keys = ("kv_p", "kv_s", "win_p", "win_s", "pool_p", "pool_s", "conv_p", "conv_s", "ffn_p", "ffn_s")
    outs = {k: [] for k in keys}
    yp_final = ys_final = None
    for l in range(depth):
        lw = _layer_weights(w_in[l], pool_w[l], cmp_pe[l], cmp_w1[l], cmp_w2[l], w_br_nsa[l])
        g_mix = norm_mix[l].reshape(1, d)
        g_ffn = norm_ffn[l].reshape(1, d)
        ps = pool_scale[l].reshape(1, POOL_W)
        cw = conv_w[l]
        wbp = w_br_pool[l].astype(MXU)
        wbc = w_br_conv[l].astype(MXU)
        wo = w_out[l].astype(MXU)
        wup = ffn_up[l].astype(MXU)
        wdn = ffn_down[l].astype(MXU)
        wcv = ffn_conv[l]
        gf = norm_final.reshape(1, d)
        final = l == depth - 1

        up, cb, ccx, sg, kvt, wint, qt, va, krows, gt = _in_proj(xp, g_mix, lw["wrow"], lw["wt"], cst_p,
                                                               tm=tm_p, seq=sp, sample=False)
        ypool, yconv = _mixers(up, cb, ccx, zeros_pool, zeros_conv, lw["pw"], ps, cw,
                               tm=tm_p, stride=1, tiles_per_seq=tps, pos0=0)
        kc, vct = _compress_prompt(kvt, perm, lw["peab"], lw["wab"], lw["w2k"], lw["w2vt"], csc_p)
        ynsa = _attention_prompt(qt, kc, vct, krows, va, gt, cov_pt, onehot_p, b=bp, s=sp)
        x1 = _merge(xp, ypool, ynsa, yconv, sg, wbp, lw["wbn"], wbc, wo, tm=tm_p)
        xp, a_last, yp_final = _conv_ffn(x1, g_ffn, zeros_ffn, wup, wcv, wdn, gf, tm=tm_p, stride=1,
                                         tiles_per_seq=tps, final=final)
        outs["kv_p"].append(kvt)
        outs["win_p"].append(wint[:, :, sp - WINDOW:])
        outs["pool_p"].append(up.reshape(bp, sp, POOL_W)[:, sp - (POOL_MAX - 1):])
        outs["conv_p"].append(ccx.reshape(bp, sp, CONV_DIM)[:, sp - (CONV_TAPS - 1):])
        outs["ffn_p"].append(a_last)

        up, cb, ccx, sg, zall = _in_proj(xs, g_mix, lw["wrow"], lw["wt"], cst_d, tm=md, seq=sd, sample=True)
        pool_pre = pool_tm[l].reshape((POOL_MAX - 1) * bd, POOL_W)
        conv_pre = jnp.transpose(state_conv[l], (1, 0, 2)).reshape((CONV_TAPS - 1) * bd, CONV_DIM)
        ffn_pre = jnp.transpose(state_ffn[l], (1, 0, 2)).reshape((CONV_TAPS - 1) * bd, D_FF)
        ypool, yconv = _mixers(up, cb, ccx, pool_pre, conv_pre, lw["pw"], ps, cw,
                               tm=md, stride=bd, tiles_per_seq=1, pos0=past)
        zrow = jnp.pad(jnp.transpose(zall, (2, 0, 1)), ((0, 0), (0, SUBLANES - sd), (0, 0)))
        wnew_t = zall[:, ATT_W + 4 * KV_W:ATT_W + 6 * KV_W, :]
        wnew_tp = jnp.pad(jnp.transpose(wnew_t, (2, 1, 0)), ((0, 0), (0, 0), (0, LANES - sd)))
        o_seq, win_new = _attention_sample(page_table, cache_t, zrow, wnew_tp, win_t, perm, lw["peab"],
                                           lw["wab"], lw["w2k"], lw["w2v"], csc_d, cov_d, expand,
                                           layer=l, pos0=past, n_new=sd)
        ynsa = jnp.transpose(o_seq.reshape(bd, Q_PER_KV, SUBLANES, KV_W)[:, :, 0:sd], (2, 0, 1, 3))
        ynsa = ynsa.reshape(md, ATT_W).astype(MXU)
        x1 = _merge(xs, ypool, ynsa, yconv, sg, wbp, lw["wbn"], wbc, wo, tm=md)
        xs, a_last, ys_final = _conv_ffn(x1, g_ffn, ffn_pre, wup, wcv, wdn, gf, tm=md, stride=bd,
                                         tiles_per_seq=1, final=final)
        outs["kv_s"].append(zall[:, ATT_W:ATT_W + 4 * KV_W, :])
        outs["win_s"].append(win_new)
        up_tm = up.reshape(sd, bd, POOL_W)
        pool_full = jnp.concatenate([pool_tm[l], up_tm], axis=0)
        outs["pool_s"].append(pool_full[pool_full.shape[0] - (POOL_MAX - 1):])
        ccx_seq = jnp.transpose(ccx.reshape(sd, bd, CONV_DIM), (1, 0, 2))
        conv_full = jnp.concatenate([state_conv[l], ccx_seq], axis=1)
        outs["conv_s"].append(conv_full[:, conv_full.shape[1] - (CONV_TAPS - 1):])
        outs["ffn_s"].append(jnp.transpose(a_last.reshape(CONV_TAPS - 1, bd, D_FF), (1, 0, 2)))

    st = lambda k: jnp.stack(outs[k])
    y_prompt = yp_final.reshape(bp, sp, d)
    y_sample = jnp.transpose(ys_final.reshape(sd, bd, d), (1, 0, 2))
    kv_prompt = _token_major(st("kv_p"), 4, N_KV, HEAD_DIM)
    kv_s = st("kv_s").reshape(depth, sd, 4, N_KV, HEAD_DIM, bd)
    kv_sample = jnp.transpose(kv_s, (0, 5, 1, 2, 3, 4))
    win_prompt = _token_major(st("win_p"), 2, N_KV, HEAD_DIM)
    win_sample = _token_major(st("win_s"), 2, N_KV, HEAD_DIM)
    pool_sample = jnp.transpose(st("pool_s"), (0, 2, 1, 3))
    return (y_prompt, y_sample, kv_prompt, kv_sample, win_prompt, win_sample, st("pool_p"), pool_sample,
            st("conv_p"), st("conv_s"), st("ffn_p"), st("ffn_s"))
```

```python
import functools
import math

import jax
import jax.numpy as jnp
from jax import lax
from jax.experimental import pallas as pl
from jax.experimental.pallas import tpu as pltpu

D_MODEL = 1024
HEAD_DIM = 64
N_HEADS = 8
N_KV = 2
Q_PER_KV = 4
ATT_W = N_HEADS * HEAD_DIM
KV_W = N_KV * HEAD_DIM
CMP_LEN = 32
CMP_STRIDE = 16
CMP_HIDDEN = 128
SEL_BLOCK = 64
SEL_TOPK = 16
N_LOCAL = 2
WINDOW = 512
Q_BLOCK = 128
FORCED_SCORE = 1e4
POOL_WINDOWS = (2, 4, 8, 16)
POOL_W = 256
POOL_MAX = 16
CONV_TAPS = 3
CONV_DIM = 256
D_FF = 2816
ROPE_THETA = 10000.0
EPS = 1e-6
PAGE = 128
IN_SPLITS = (POOL_W, ATT_W, 6 * KV_W, 3 * N_HEADS, CONV_DIM, CONV_DIM, CONV_DIM, D_MODEL, D_MODEL, D_MODEL)

LANES = 128
SUBLANES = 8
BF16_ROWS = 16
V7X_VMEM_BYTES = 64 * 1024 * 1024
VMEM_CAP = 56 * 1024 * 1024

NEG = -1e30
MXU = jnp.bfloat16
F32 = jnp.float32
LOG2E = math.log2(math.e)

W_ROW = POOL_W + 3 * CONV_DIM + 3 * D_MODEL
T_Q = 2 * ATT_W
T_KV = 5 * KV_W
T_WIN = 3 * KV_W
W_T = T_Q + T_KV + T_WIN + LANES
VA_ROWS = HEAD_DIM + BF16_ROWS
Z_ALL = ATT_W + 4 * KV_W + 2 * KV_W + LANES
FF_CHUNK = 256


def _cparams(est_bytes, ndim=1):
    limit = int(min(max(est_bytes, 16 * 1024 * 1024), VMEM_CAP))
    return pltpu.CompilerParams(dimension_semantics=("arbitrary",) * ndim, vmem_limit_bytes=limit)


def _resident(shape):
    n = len(shape)
    return pl.BlockSpec(shape, lambda *_: (0,) * n, pipeline_mode=pl.Buffered(1))


def _sigmoid(x):
    return 1.0 / (1.0 + jnp.exp(-x))


def _dot(a, b):
    return jnp.dot(a, b, preferred_element_type=F32)


def _dot_nt(a, b):
    return lax.dot_general(a, b, (((1,), (1,)), ((), ())), preferred_element_type=F32)


def _split3(x):
    hi = x.astype(MXU)
    r1 = x - hi.astype(F32)
    mid = r1.astype(MXU)
    lo = (r1 - mid.astype(F32)).astype(MXU)
    return hi, mid, lo


def _dot3(x, w):
    out = None
    for term in _split3(x):
        t = _dot(term, w)
        out = t if out is None else out + t
    return out


def _inproj_kernel(x_ref, g_ref, wrow_ref, wt_ref, cst_ref, up_ref, cb_ref, ccx_ref, sg_ref, *outs, tm, sample):
    x = x_ref[...]
    h = x * lax.rsqrt(jnp.mean(x * x, axis=-1, keepdims=True) + EPS)
    h = (h * g_ref[...]).astype(MXU)

    up_ref[...] = _dot(h, wrow_ref[:, 0:POOL_W])
    z = _dot(h, wrow_ref[:, POOL_W:POOL_W + 3 * CONV_DIM])
    cb_ref[...] = z[:, 0:256]
    ccx_ref[...] = z[:, 256:512] * z[:, 512:768]
    g0 = POOL_W + 3 * CONV_DIM
    for j in range(6):
        z = _dot(h, wrow_ref[:, g0 + 512 * j:g0 + 512 * (j + 1)])
        sg_ref[:, 512 * j:512 * (j + 1)] = _sigmoid(z).astype(sg_ref.dtype)

    cost = cst_ref[0:LANES, :]
    sint = cst_ref[LANES:2 * LANES, :]
    zq = _dot_nt(wt_ref[0:T_Q, :], h)
    q = jnp.concatenate(
        [zq[i * LANES:(i + 1) * LANES] * cost + zq[ATT_W + i * LANES:ATT_W + (i + 1) * LANES] * sint
         for i in range(ATT_W // LANES)], axis=0)
    zk = _dot_nt(wt_ref[T_Q:T_Q + T_KV, :], h)
    ksel = zk[256:384] * cost + zk[384:512] * sint
    kv = jnp.concatenate([zk[0:256], ksel, zk[512:640]], axis=0)
    zw = _dot_nt(wt_ref[T_Q + T_KV:T_Q + T_KV + T_WIN, :], h)
    kwin = zw[0:128] * cost + zw[128:256] * sint
    win = jnp.concatenate([kwin, zw[256:384]], axis=0)
    gates = _sigmoid(_dot_nt(wt_ref[T_Q + T_KV + T_WIN:W_T, :], h))

    if sample:
        (zall_ref,) = outs
        for c in range(tm // LANES):
            cols = slice(c * LANES, (c + 1) * LANES)
            zall_ref[c] = jnp.concatenate([q[:, cols], kv[:, cols], win[:, cols], gates[:, cols]], axis=0)
    else:
        kvt_ref, wint_ref, qt_ref, va_ref, krows_ref, gt_ref = outs
        kvt_ref[0] = kv
        wint_ref[0] = win
        ones = jnp.ones((BF16_ROWS, LANES), va_ref.dtype)
        vsel = kv[384:512].astype(va_ref.dtype)
        vwin = win[128:256].astype(va_ref.dtype)
        for c in range(tm // LANES):
            cols = slice(c * LANES, (c + 1) * LANES)
            qt_ref[c] = q[:, cols].astype(qt_ref.dtype)
            va_ref[c] = jnp.concatenate(
                [vsel[0:64, cols], ones, vsel[64:128, cols], ones, vwin[0:64, cols], ones, vwin[64:128, cols], ones],
                axis=0)
            gt_ref[c] = gates[:, cols]
        krows_ref[:, 0:KV_W] = ksel.T.astype(krows_ref.dtype)
        krows_ref[:, KV_W:2 * KV_W] = kwin.T.astype(krows_ref.dtype)


def _in_proj(x, g, wrow, wt, cst, *, tm, seq, sample):
    m = x.shape[0]
    nt = m // tm
    ntab = cst.shape[1] // tm
    nb = tm // LANES
    tps = seq // tm if not sample else 1
    row = lambda w: pl.BlockSpec((tm, w), lambda i: (i, 0))
    blk = lambda r: pl.BlockSpec((nb, r, LANES), lambda i: (i, 0, 0))
    out_shape = [jax.ShapeDtypeStruct((m, POOL_W), F32), jax.ShapeDtypeStruct((m, CONV_DIM), F32),
                 jax.ShapeDtypeStruct((m, CONV_DIM), F32), jax.ShapeDtypeStruct((m, 3 * D_MODEL), MXU)]
    out_specs = [row(POOL_W), row(CONV_DIM), row(CONV_DIM), row(3 * D_MODEL)]
    if sample:
        out_shape += [jax.ShapeDtypeStruct((m // LANES, Z_ALL, LANES), F32)]
        out_specs += [blk(Z_ALL)]
    else:
        out_shape += [jax.ShapeDtypeStruct((m // seq, 4 * KV_W, seq), F32),
                      jax.ShapeDtypeStruct((m // seq, 2 * KV_W, seq), F32),
                      jax.ShapeDtypeStruct((m // LANES, ATT_W, LANES), MXU),
                      jax.ShapeDtypeStruct((m // LANES, 4 * VA_ROWS, LANES), MXU),
                      jax.ShapeDtypeStruct((m, 2 * KV_W), MXU),
                      jax.ShapeDtypeStruct((m // LANES, LANES, LANES), F32)]
        out_specs += [pl.BlockSpec((1, 4 * KV_W, tm), lambda i: (i // tps, 0, i % tps)),
                      pl.BlockSpec((1, 2 * KV_W, tm), lambda i: (i // tps, 0, i % tps)),
                      blk(ATT_W), blk(4 * VA_ROWS), row(2 * KV_W), blk(LANES)]
    est = (2 * tm * D_MODEL * 4 + wrow.size * 2 + wt.size * 2 + 4 * tm * 256 * 4
           + 2 * tm * (3 * 256 * 4 + 3072 * 2) + 2 * tm * Z_ALL * 4 + tm * (T_Q + T_KV + T_WIN + 1024 + 1024) * 4
           + (4 << 20))
    return pl.pallas_call(
        functools.partial(_inproj_kernel, tm=tm, sample=sample),
        grid=(nt,),
        in_specs=[pl.BlockSpec((tm, D_MODEL), lambda i: (i, 0)),
                  pl.BlockSpec((1, D_MODEL), lambda i: (0, 0)),
                  _resident((D_MODEL, W_ROW)),
                  _resident((W_T, D_MODEL)),
                  pl.BlockSpec((2 * LANES, tm), lambda i: (0, i % ntab))],
        out_specs=tuple(out_specs),
        out_shape=tuple(out_shape),
        compiler_params=_cparams(est),
        name="in_proj",
    )(x, g, wrow, wt, cst)


def _mix_kernel(up_ref, cb_ref, ccx_ref, ppre_ref, cpre_ref, pw_ref, ps_ref, cw_ref,
                yp_ref, yc_ref, pext, cext, *, tm, stride, tiles_per_seq, pos0):
    i = pl.program_id(0)
    hp = pext.shape[0] - tm
    hc = cext.shape[0] - tm

    @pl.when(i % tiles_per_seq == 0)
    def _():
        pext[0:hp, :] = ppre_ref[...]
        cext[0:hc, :] = cpre_ref[...]

    u = up_ref[...]
    pext[hp:hp + tm, :] = u
    acc = u
    sums = {}
    for k in range(1, POOL_MAX):
        acc = acc + pext[hp - k * stride:hp - k * stride + tm, :]
        if k + 1 in POOL_WINDOWS:
            sums[k + 1] = acc
    row = lax.broadcasted_iota(jnp.int32, (tm, 1), 0)
    if stride == 1:
        t_abs = pos0 + (i % tiles_per_seq) * tm + row
    else:
        t_abs = pos0 + lax.shift_right_logical(row, stride.bit_length() - 1)
    tp1 = (t_abs + 1).astype(F32)
    lane = lax.broadcasted_iota(jnp.int32, (1, POOL_W), 1)
    grp = POOL_W // len(POOL_WINDOWS)
    mean = None
    for gi, w in reversed(list(enumerate(POOL_WINDOWS))):
        mw = sums[w] / jnp.minimum(float(w), tp1)
        mean = mw if mean is None else jnp.where(lane < (gi + 1) * grp, mw, mean)
    pooled = mean - u
    y = _dot(pooled.astype(MXU), pw_ref[...]) * ps_ref[...]
    yp_ref[...] = y.astype(yp_ref.dtype)

    e0 = ccx_ref[...]
    cext[hc:hc + tm, :] = e0
    e1 = cext[hc - stride:hc - stride + tm, :]
    e2 = cext[hc - 2 * stride:hc - 2 * stride + tm, :]
    conv = cw_ref[0:1, :] * e2 + cw_ref[1:2, :] * e1 + cw_ref[2:3, :] * e0
    yc_ref[...] = (cb_ref[...] * conv).astype(yc_ref.dtype)

    if tiles_per_seq > 1:
        pext[0:hp, :] = pext[tm:tm + hp, :]
        cext[0:hc, :] = cext[tm:tm + hc, :]


def _mixers(up, cb, ccx, ppre, cpre, pw, ps, cw, *, tm, stride, tiles_per_seq, pos0):
    m = up.shape[0]
    nt = m // tm
    hp = ppre.shape[0] // (nt // tiles_per_seq)
    hc = cpre.shape[0] // (nt // tiles_per_seq)
    row = pl.BlockSpec((tm, 256), lambda i: (i, 0))
    est = 2 * 5 * tm * 256 * 4 + (2 * tm + hp + hc) * 256 * 4 + 2 * (hp + hc) * 256 * 4 + 12 * tm * 256 * 4 + (2 << 20)
    return pl.pallas_call(
        functools.partial(_mix_kernel, tm=tm, stride=stride, tiles_per_seq=tiles_per_seq, pos0=pos0),
        grid=(nt,),
        in_specs=[row, row, row,
                  pl.BlockSpec((hp, 256), lambda i: (i // tiles_per_seq, 0)),
                  pl.BlockSpec((hc, 256), lambda i: (i // tiles_per_seq, 0)),
                  pl.BlockSpec((256, 256), lambda i: (0, 0)),
                  pl.BlockSpec((1, 256), lambda i: (0, 0)),
                  pl.BlockSpec((CONV_TAPS, 256), lambda i: (0, 0))],
        out_specs=(row, row),
        out_shape=(jax.ShapeDtypeStruct((m, 256), MXU), jax.ShapeDtypeStruct((m, 256), MXU)),
        scratch_shapes=[pltpu.VMEM((hp + tm, 256), F32), pltpu.VMEM((hc + tm, 256), F32)],
        compiler_params=_cparams(est),
        name="mixers",
    )(up, cb, ccx, ppre, cpre, pw, ps, cw)


def _gelu_tanh(x):
    return x * (0.5 * (1.0 + jnp.tanh(0.7978845608028654 * (x + 0.044715 * (x * x * x)))))


def _compress_bias(peab_ref, wab_ref):
    pe3 = _split3(peab_ref[...])
    out = []
    for s in range(2):
        ba = None
        bb = None
        for term in pe3:
            ta = _dot(term, wab_ref[s, 0])
            tb = _dot(term, wab_ref[s, 1])
            ba = ta if ba is None else ba + ta
            bb = tb if bb is None else bb + tb
        out.append(ba[2 * s:2 * s + 1, :] + bb[2 * s + 1:2 * s + 2, :])
    return out


def _compress_hidden(get_page, n_pages, perm_ref, wab_ref, bias, pb_scr):
    n = n_pages * (PAGE // CMP_STRIDE)
    per = PAGE // CMP_STRIDE
    ys = ([], [])
    for p in range(n_pages):
        yp = _dot_nt(perm_ref[...], get_page(p).astype(MXU))
        for s in range(2):
            ys[s].append(jnp.concatenate(
                [yp[t * per:(t + 1) * per, s * KV_W:(s + 1) * KV_W] for t in range(CMP_STRIDE)], axis=1))
    acts = []
    for s in range(2):
        rows = jnp.concatenate(ys[s], axis=0).astype(MXU)
        pa = _dot(rows, wab_ref[s, 0])
        pb = _dot(rows, wab_ref[s, 1])
        pb_scr[0:n, :] = pb
        pb_scr[n:n + SUBLANES, :] = jnp.zeros((SUBLANES, 2 * CMP_HIDDEN), F32)
        pre = pa + pb_scr[1:n + 1, :] + bias[s]
        acts.append(_gelu_tanh(pre).astype(MXU))
    return acts


def _cmp_prompt_kernel(kvt_ref, perm_ref, peab_ref, wab_ref, w2k_ref, w2vt_ref, csc_ref, kc_ref, vct_ref, pb_scr,
                       *, n_pages):
    def get_page(p):
        return kvt_ref[0, :, p * PAGE:(p + 1) * PAGE]

    bias = _compress_bias(peab_ref, wab_ref)
    act_k, act_v = _compress_hidden(get_page, n_pages, perm_ref, wab_ref, bias, pb_scr)
    yk = _dot(act_k, w2k_ref[...])
    kc = yk[:, 0:LANES] * csc_ref[:, 0:LANES] + yk[:, LANES:2 * LANES] * csc_ref[:, LANES:2 * LANES]
    kc_ref[0] = kc.astype(kc_ref.dtype)
    vct_ref[0] = _dot_nt(w2vt_ref[...], act_v).astype(vct_ref.dtype)


def _cmp_sample_kernel(pt_ref, *refs, n_pages, nb):
    del pt_ref
    page_refs = refs[0:nb * n_pages]
    perm_ref, peab_ref, wab_ref, w2k_ref, w2v_ref, csc_ref, kc_ref, vc_ref, pb_scr, bias_scr = refs[nb * n_pages:]
    n = n_pages * (PAGE // CMP_STRIDE)

    @pl.when(pl.program_id(0) == 0)
    def _():
        for s, b in enumerate(_compress_bias(peab_ref, wab_ref)):
            bias_scr[s] = jnp.broadcast_to(b, (SUBLANES, 2 * CMP_HIDDEN))

    bias = [bias_scr[s, 0:1, :] for s in range(2)]
    act_k, act_v = _compress_hidden(lambda p: page_refs[p][0, 0], nb * n_pages, perm_ref, wab_ref, bias, pb_scr)
    yk = _dot(act_k, w2k_ref[...])
    vc = _dot(act_v, w2v_ref[...])
    for j in range(nb):
        r = slice(j * n, (j + 1) * n)
        kc = yk[r, 0:LANES] * csc_ref[:, 0:LANES] + yk[r, LANES:2 * LANES] * csc_ref[:, LANES:2 * LANES]
        kc_ref[j] = kc.astype(kc_ref.dtype)
        vc_ref[j] = vc[r].astype(vc_ref.dtype)


def _compress_sample(page_table, cache_t, perm, peab, wab, w2k, w2v, csc, *, layer, nb):
    bd, n_pages = page_table.shape
    n = n_pages * (PAGE // CMP_STRIDE)
    in_specs = ([pl.BlockSpec((1, 1, 2 * KV_W, PAGE), lambda i, pt, j=j, k=k: (layer, pt[i * nb + j, k], 0, 0))
                 for j in range(nb) for k in range(n_pages)]
                + [pl.BlockSpec((PAGE, PAGE), lambda i, pt: (0, 0)),
                   pl.BlockSpec((SUBLANES, 2048), lambda i, pt: (0, 0)),
                   pl.BlockSpec((2, 2, 2048, 256), lambda i, pt: (0, 0, 0, 0), pipeline_mode=pl.Buffered(1)),
                   pl.BlockSpec((256, 256), lambda i, pt: (0, 0)),
                   pl.BlockSpec((256, 128), lambda i, pt: (0, 0)),
                   pl.BlockSpec((n, 256), lambda i, pt: (0, 0))])
    out = pl.BlockSpec((nb, n, LANES), lambda i, pt: (i, 0, 0))
    est = 2 * nb * n_pages * 2 * KV_W * PAGE * 4 + wab.size * 2 + 10 * nb * n * 2048 * 4 + (4 << 20)
    return pl.pallas_call(
        functools.partial(_cmp_sample_kernel, n_pages=n_pages, nb=nb),
        grid_spec=pltpu.PrefetchScalarGridSpec(
            num_scalar_prefetch=1,
            grid=(bd // nb,),
            in_specs=in_specs,
            out_specs=(out, out),
            scratch_shapes=[pltpu.VMEM((nb * n + SUBLANES, 2 * CMP_HIDDEN), F32),
                            pltpu.VMEM((2, SUBLANES, 2 * CMP_HIDDEN), F32)]),
        out_shape=(jax.ShapeDtypeStruct((bd, n, LANES), MXU), jax.ShapeDtypeStruct((bd, n, LANES), MXU)),
        compiler_params=_cparams(est),
        name="compress_sample",
    )(page_table, *([cache_t] * (nb * n_pages)), perm, peab, wab, w2k, w2v, csc)


def _compress_prompt(kvt, perm, peab, wab, w2k, w2vt, csc):
    b, _, s = kvt.shape
    n_pages = s // PAGE
    n = s // CMP_STRIDE
    est = 2 * 256 * s * 4 + wab.size * 2 + 8 * n * 2048 * 4 + (4 << 20)
    return pl.pallas_call(
        functools.partial(_cmp_prompt_kernel, n_pages=n_pages),
        grid=(b,),
        in_specs=[pl.BlockSpec((1, 2 * KV_W, s), lambda i: (i, 0, 0)),
                  pl.BlockSpec((PAGE, PAGE), lambda i: (0, 0)),
                  pl.BlockSpec((SUBLANES, 2048), lambda i: (0, 0)),
                  _resident((2, 2, 2048, 256)),
                  pl.BlockSpec((256, 256), lambda i: (0, 0)),
                  pl.BlockSpec((128, 256), lambda i: (0, 0)),
                  pl.BlockSpec((n, 256), lambda i: (0, 0))],
        out_specs=(pl.BlockSpec((1, n, LANES), lambda i: (i, 0, 0)),
                   pl.BlockSpec((1, LANES, n), lambda i: (i, 0, 0))),
        out_shape=(jax.ShapeDtypeStruct((b, n, LANES), MXU), jax.ShapeDtypeStruct((b, LANES, n), MXU)),
        scratch_shapes=[pltpu.VMEM((n + SUBLANES, 2 * CMP_HIDDEN), F32)],
        compiler_params=_cparams(est),
        name="compress_prompt",
    )(kvt, perm, peab, wab, w2k, w2vt, csc)


def _attn_prompt_kernel(qt_ref, kc_ref, vct_ref, krows_ref, va_ref, gt_ref, cov_ref, oh_ref,
                        y_ref, qaug, score_scr, *, n_cmp, n_sel):
    qb = pl.program_id(1)
    q0 = qb * Q_BLOCK
    nq = N_HEADS * Q_BLOCK
    half = Q_PER_KV * Q_BLOCK

    qt = qt_ref[0]
    rsel = lax.broadcasted_iota(jnp.int32, (2 * HEAD_DIM, 1), 0) < HEAD_DIM
    zero = jnp.zeros((), qt.dtype)
    for g in range(N_KV):
        for r in range(Q_PER_KV):
            blk = qt[r * LANES:(r + 1) * LANES, :]
            keep = rsel if g == 0 else jnp.logical_not(rsel)
            qaug[0:LANES, (g * Q_PER_KV + r) * Q_BLOCK:(g * Q_PER_KV + r + 1) * Q_BLOCK] = jnp.where(keep, blk, zero)
    qbd = qaug[0:LANES, :]

    lane = lax.broadcasted_iota(jnp.int32, (1, nq), 1)
    qpos = q0 + jnp.bitwise_and(lane, Q_BLOCK - 1)

    sc = _dot(kc_ref[0], qbd)
    posc = lax.broadcasted_iota(jnp.int32, (n_cmp, 1), 0) * CMP_STRIDE + (CMP_LEN - 1)
    valid = posc <= qpos
    s = jnp.where(valid, sc, NEG)
    mx = jnp.max(s, axis=0, keepdims=True)
    mx = jnp.where(mx > 0.5 * NEG, mx, 0.0)
    e = jnp.where(valid, jnp.exp2(s - mx), 0.0)
    p = e / jnp.maximum(jnp.sum(e, axis=0, keepdims=True), 1.0)
    pm = p.astype(MXU)
    vct = vct_ref[0]
    o_cmp = [_dot(vct[g * HEAD_DIM:(g + 1) * HEAD_DIM, :], pm[:, g * half:(g + 1) * half]) for g in range(N_KV)]

    w2 = N_KV * Q_BLOCK
    jrow = lax.broadcasted_iota(jnp.int32, (n_sel, 1), 0)
    qp2 = q0 + jnp.bitwise_and(lax.broadcasted_iota(jnp.int32, (1, w2), 1), Q_BLOCK - 1)
    cur = lax.shift_right_logical(qp2, 6)
    forced = (jrow == 0) | ((jrow <= cur) & (jrow > cur - N_LOCAL))
    causal = jrow * SEL_BLOCK <= qp2
    psum = []
    for g in range(N_KV):
        ps = p[:, g * half:g * half + Q_BLOCK]
        for r in range(1, Q_PER_KV):
            ps = ps + p[:, g * half + r * Q_BLOCK:g * half + (r + 1) * Q_BLOCK]
        psum.append(ps)
    imp = None
    for term in _split3(jnp.concatenate(psum, axis=1)):
        t = _dot(cov_ref[...], term)
        imp = t if imp is None else imp + t
    score = jnp.where(forced, FORCED_SCORE, jnp.where(causal, imp, -1.0))
    score_scr[...] = score
    n_live = jnp.minimum(n_sel, 2 * qb + 2)

    rank_unroll = 4

    def rank_body(i, cnt):
        for u in range(rank_unroll):
            jp = i * rank_unroll + u
            rowv = score_scr[pl.ds(jp, 1), :]
            ahead = (rowv > score) | ((rowv == score) & (jp < jrow))
            cnt = cnt + jnp.where(ahead, 1.0, 0.0)
        return cnt

    cnt = lax.fori_loop(0, (n_live + rank_unroll - 1) // rank_unroll, rank_body, jnp.zeros((n_sel, w2), F32))
    bias = jnp.where((cnt < float(SEL_TOPK)) & causal, 0.0, NEG).astype(MXU)
    qaug[LANES:2 * LANES, :] = jnp.zeros((LANES, nq), MXU)
    qaug[LANES:LANES + n_sel, :] = jnp.concatenate(
        [bias[:, 0:Q_BLOCK]] * Q_PER_KV + [bias[:, Q_BLOCK:w2]] * Q_PER_KV, axis=1)

    def step(state, k0, kk, rhs, vrow0, visible):
        m, a0, a1 = state
        nk = kk.shape[0]
        kp = k0 + lax.broadcasted_iota(jnp.int32, (nk, 1), 0)
        sij = jnp.where(visible(kp), _dot(kk, rhs), NEG)
        m_new = jnp.maximum(m, jnp.max(sij, axis=0, keepdims=True))
        alpha = jnp.exp2(m - m_new)
        pij = jnp.exp2(sij - m_new).astype(MXU)
        vblk = jnp.concatenate([va_ref[k0 // LANES + sub][vrow0:vrow0 + 2 * VA_ROWS, :]
                                for sub in range(nk // LANES)], axis=1)
        a0 = alpha[:, 0:half] * a0 + _dot(vblk[0:VA_ROWS], pij[:, 0:half])
        a1 = alpha[:, half:nq] * a1 + _dot(vblk[VA_ROWS:2 * VA_ROWS], pij[:, half:nq])
        return m_new, a0, a1

    def merge(sa, sb):
        ma, a0a, a1a = sa
        mb, a0b, a1b = sb
        mm = jnp.maximum(ma, mb)
        wa = jnp.exp2(ma - mm)
        wb = jnp.exp2(mb - mm)
        a0 = wa[:, 0:half] * a0a + wb[:, 0:half] * a0b
        a1 = wa[:, half:nq] * a1a + wb[:, half:nq] * a1b
        return [a[0:HEAD_DIM] * (1.0 / a[HEAD_DIM:HEAD_DIM + 1]) for a in (a0, a1)]

    init = (jnp.full((1, nq), NEG, F32), jnp.zeros((VA_ROWS, half), F32), jnp.zeros((VA_ROWS, half), F32))

    sc_keys = 2 * Q_BLOCK

    def sel_step(state, c):
        k0 = pl.multiple_of(c * sc_keys, sc_keys)
        kk = jnp.concatenate([krows_ref[pl.ds(k0, sc_keys), 0:KV_W], oh_ref[pl.ds(k0, sc_keys), :]], axis=1)
        return step(state, k0, kk, qaug[...], 0, lambda kp: kp <= qpos)

    n_chunks = qb // 2 + 1
    sa, sb = lax.fori_loop(0, (n_chunks + 1) // 2,
                           lambda i, st: (sel_step(st[0], 2 * i), sel_step(st[1], 2 * i + 1)), (init, init))
    sel_o = merge(sa, sb)

    n_back = WINDOW // Q_BLOCK

    def win_step(state, c):
        blk = qb - n_back + c
        ok = blk >= 0
        k0 = pl.multiple_of(jnp.maximum(blk, 0) * Q_BLOCK, Q_BLOCK)
        kk = krows_ref[pl.ds(k0, Q_BLOCK), KV_W:2 * KV_W]
        return step(state, k0, kk, qbd, 2 * VA_ROWS, lambda kp: (kp <= qpos) & (kp >= qpos - WINDOW) & ok)

    sa = init
    sb = init
    for c in range(n_back + 1):
        if c % 2 == 0:
            sa = win_step(sa, c)
        else:
            sb = win_step(sb, c)
    win_o = merge(sa, sb)

    gt = gt_ref[0]
    for r in range(Q_PER_KV):
        c = slice(r * Q_BLOCK, (r + 1) * Q_BLOCK)
        pair = []
        for g in range(N_KV):
            hh = g * Q_PER_KV + r
            pair.append(o_cmp[g][:, c] * gt[hh:hh + 1, :] + sel_o[g][:, c] * gt[N_HEADS + hh:N_HEADS + hh + 1, :]
                        + win_o[g][:, c] * gt[2 * N_HEADS + hh:2 * N_HEADS + hh + 1, :])
        y_ref[:, r * LANES:(r + 1) * LANES] = jnp.concatenate(pair, axis=0).T.astype(y_ref.dtype)


def _attention_prompt(qt, kc, vct, krows, va, gt, cov, onehot, *, b, s):
    nqb = s // Q_BLOCK
    n_cmp = kc.shape[1]
    n_sel = s // SEL_BLOCK
    est = (2 * s * 2 * KV_W * 2 + 2 * (s // LANES) * 4 * VA_ROWS * LANES * 2 + s * LANES * 2 + 10 * 256 * 1024 * 4
           + 5 * n_cmp * 1024 * 4 + (6 << 20))
    return pl.pallas_call(
        functools.partial(_attn_prompt_kernel, n_cmp=n_cmp, n_sel=n_sel),
        grid=(b, nqb),
        in_specs=[pl.BlockSpec((1, ATT_W, LANES), lambda i, j: (i * nqb + j, 0, 0)),
                  pl.BlockSpec((1, n_cmp, LANES), lambda i, j: (i, 0, 0)),
                  pl.BlockSpec((1, LANES, n_cmp), lambda i, j: (i, 0, 0)),
                  pl.BlockSpec((s, 2 * KV_W), lambda i, j: (i, 0)),
                  pl.BlockSpec((s // LANES, 4 * VA_ROWS, LANES), lambda i, j: (i, 0, 0)),
                  pl.BlockSpec((1, LANES, LANES), lambda i, j: (i * nqb + j, 0, 0)),
                  pl.BlockSpec((n_sel, n_cmp), lambda i, j: (0, 0)),
                  _resident((s, LANES))],
        out_specs=pl.BlockSpec((Q_BLOCK, ATT_W), lambda i, j: (i * nqb + j, 0)),
        out_shape=jax.ShapeDtypeStruct((b * s, ATT_W), MXU),
        scratch_shapes=[pltpu.VMEM((2 * LANES, N_HEADS * Q_BLOCK), MXU), pltpu.VMEM((n_sel, N_KV * Q_BLOCK), F32)],
        compiler_params=_cparams(est, 2),
        name="attention_prompt",
    )(qt, kc, vct, krows, va, gt, cov, onehot)


def _attn_sample_kernel(pt_ref, *refs, n_pages, ns, n_new, pos0, n_sel, win_past):
    del pt_ref
    rest = refs[ns * n_pages:]
    for j in range(ns):
        _attn_sample_one(j, refs[j * n_pages:(j + 1) * n_pages], *rest, n_pages=n_pages, n_new=n_new, pos0=pos0,
                         n_sel=n_sel, win_past=win_past)


def _attn_sample_one(j, page_refs, zrow_ref, wnewt_ref, wpast_ref, kc_ref, vc_ref, cov_ref, exp_ref, o_ref, wout_ref,
                     *, n_pages, n_new, pos0, n_sel, win_past):
    n = n_pages * (PAGE // CMP_STRIDE)
    rows = Q_PER_KV * SUBLANES
    past = n_pages * PAGE

    kc = kc_ref[j]
    vc = vc_ref[j]
    kselt = jnp.concatenate([pr[0, 0, 0:KV_W, :] for pr in page_refs], axis=1).astype(MXU)
    vselt = jnp.concatenate([pr[0, 0, KV_W:2 * KV_W, :] for pr in page_refs], axis=1).astype(MXU)
    wpast = wpast_ref[0, j]
    kwint = wpast[0:KV_W].astype(MXU)
    vwint = wpast[KV_W:2 * KV_W].astype(MXU)
    zrow = zrow_ref[j]
    knew = zrow[:, ATT_W:ATT_W + 4 * KV_W]
    wnew = zrow[:, ATT_W + 4 * KV_W:ATT_W + 6 * KV_W]
    g_off = ATT_W + 6 * KV_W

    ri = lax.broadcasted_iota(jnp.int32, (rows, 1), 0)
    step = jnp.bitwise_and(ri, SUBLANES - 1)
    qp = pos0 + step
    lane_c = lax.broadcasted_iota(jnp.int32, (1, n), 1)
    valid_c = lane_c * CMP_STRIDE + (CMP_LEN - 1) <= qp
    jl = lax.broadcasted_iota(jnp.int32, (1, LANES), 1)
    cur = lax.shift_right_logical(qp, 6)
    forced = (jl == 0) | ((jl <= cur) & (jl > cur - N_LOCAL))
    causal = (jl * SEL_BLOCK <= qp) & (jl < n_sel)
    lane_h = lax.broadcasted_iota(jnp.int32, (1, KV_W), 1)
    wl = lax.broadcasted_iota(jnp.int32, (1, win_past), 1)
    wpos = pos0 - win_past + wl
    valid_w = (wpos >= 0) & (wpos <= qp) & (wpos >= qp - WINDOW)

    def new_keys(qf, krows, vrows, ok_fn):
        out = []
        for tk in range(n_new):
            sv = jnp.sum(qf * krows[tk:tk + 1, :], axis=1, keepdims=True)
            out.append((jnp.where(ok_fn(tk), sv, NEG), vrows[tk:tk + 1, :]))
        return out

    def softmax_two(s_past, extra, v_past_t):
        mx = jnp.max(s_past, axis=1, keepdims=True)
        for sv, _ in extra:
            mx = jnp.maximum(mx, sv)
        pp = jnp.exp2(s_past - mx)
        den = jnp.sum(pp, axis=1, keepdims=True)
        acc = _dot_nt(pp.astype(MXU), v_past_t)
        for sv, vv in extra:
            pe = jnp.exp2(sv - mx)
            den = den + pe
            acc = acc + pe * vv
        return acc / den

    total = jnp.zeros((rows, KV_W), F32)
    for g in range(N_KV):
        head_lanes = (lane_h >= g * HEAD_DIM) & (lane_h < (g + 1) * HEAD_DIM)
        qf = jnp.concatenate([jnp.where(head_lanes, zrow[:, r * LANES:(r + 1) * LANES], 0.0)
                              for r in range(Q_PER_KV)], axis=0)
        q = qf.astype(MXU)
        gate = [jnp.concatenate([zrow[:, g_off + k * N_HEADS + g * Q_PER_KV + r:
                                      g_off + k * N_HEADS + g * Q_PER_KV + r + 1] for r in range(Q_PER_KV)], axis=0)
                for k in range(3)]

        sc = _dot_nt(q, kc)
        s = jnp.where(valid_c, sc, NEG)
        mx = jnp.max(s, axis=1, keepdims=True)
        mx = jnp.where(mx > 0.5 * NEG, mx, 0.0)
        e = jnp.where(valid_c, jnp.exp2(s - mx), 0.0)
        p = e / jnp.maximum(jnp.sum(e, axis=1, keepdims=True), 1.0)
        o_c = _dot(p.astype(MXU), vc)

        ps = p
        for r in range(1, Q_PER_KV):
            ps = ps + pltpu.roll(p, r * SUBLANES, 0)
        imp = _dot3(ps, cov_ref[...])
        score = jnp.where(forced, FORCED_SCORE, jnp.where(causal, imp, -1.0))
        cnt = jnp.zeros((rows, LANES), F32)
        for jp in range(n_sel):
            col = score[:, jp:jp + 1]
            cnt = cnt + jnp.where((col > score) | ((col == score) & (jp < jl)), 1.0, 0.0)
        sel = ((cnt < float(SEL_TOPK)) & causal).astype(F32)
        keymask = _dot(sel.astype(MXU), exp_ref[...])
        s_sel = jnp.where(keymask > 0.5, _dot(q, kselt), NEG)
        new_blk = past // SEL_BLOCK
        sel_new = sel[:, new_blk:new_blk + 1] > 0.5
        extra = new_keys(qf, knew[:, 2 * KV_W:3 * KV_W], knew[:, 3 * KV_W:4 * KV_W],
                         lambda tk: sel_new & (tk <= step))
        o_s = softmax_two(s_sel, extra, vselt)

        s_win = jnp.where(valid_w, _dot(q, kwint), NEG)
        extra = new_keys(qf, wnew[:, 0:KV_W], wnew[:, KV_W:2 * KV_W], lambda tk: tk <= step)
        o_w = softmax_two(s_win, extra, vwint)

        o = o_c * gate[0] + o_s * gate[1] + o_w * gate[2]
        total = total + jnp.where(head_lanes, o, 0.0)
    o_ref[j] = total

    rolled = pltpu.roll(wpast, win_past - n_new, 1)
    newr = pltpu.roll(wnewt_ref[j], LANES - n_new, 1)
    wout_ref[j, :, 0:win_past - LANES] = rolled[:, 0:win_past - LANES]
    wout_ref[j, :, win_past - LANES:win_past] = jnp.where(jl >= LANES - n_new, newr,
                                                          rolled[:, win_past - LANES:win_past])


def _attention_sample(page_table, cache_t, zrow, wnewt, wpast_t, kc, vc, cov, expand, *, layer, ns, pos0, n_new):
    bd, n_pages = page_table.shape
    n = n_pages * (PAGE // CMP_STRIDE)
    past = n_pages * PAGE
    n_sel = -(-(past + n_new) // SEL_BLOCK)
    win_past = wpast_t.shape[3]
    rows = Q_PER_KV * SUBLANES

    in_specs = ([pl.BlockSpec((1, 1, 2 * KV_W, PAGE), lambda i, pt, j=j, k=k: (layer, pt[i * ns + j, k], 1, 0))
                 for j in range(ns) for k in range(n_pages)]
                + [pl.BlockSpec((ns, SUBLANES, Z_ALL), lambda i, pt: (i, 0, 0)),
                   pl.BlockSpec((ns, 2 * KV_W, LANES), lambda i, pt: (i, 0, 0)),
                   pl.BlockSpec((1, ns, 2 * KV_W, win_past), lambda i, pt: (layer, i, 0, 0)),
                   pl.BlockSpec((ns, n, LANES), lambda i, pt: (i, 0, 0)),
                   pl.BlockSpec((ns, n, LANES), lambda i, pt: (i, 0, 0)),
                   pl.BlockSpec((n, LANES), lambda i, pt: (0, 0)),
                   pl.BlockSpec((LANES, past), lambda i, pt: (0, 0))])
    est = (2 * ns * n_pages * 2 * KV_W * PAGE * 4 + ns * 8 * past * KV_W * 4 + 8 * ns * 2 * KV_W * win_past * 4
           + (6 << 20))
    return pl.pallas_call(
        functools.partial(_attn_sample_kernel, n_pages=n_pages, ns=ns, n_new=n_new, pos0=pos0, n_sel=n_sel,
                          win_past=win_past),
        grid_spec=pltpu.PrefetchScalarGridSpec(
            num_scalar_prefetch=1,
            grid=(bd // ns,),
            in_specs=in_specs,
            out_specs=(pl.BlockSpec((ns, rows, KV_W), lambda i, pt: (i, 0, 0)),
                       pl.BlockSpec((ns, 2 * KV_W, win_past), lambda i, pt: (i, 0, 0)))),
        out_shape=(jax.ShapeDtypeStruct((bd, rows, KV_W), F32),
                   jax.ShapeDtypeStruct((bd, 2 * KV_W, win_past), F32)),
        compiler_params=_cparams(est),
        name="attention_sample",
    )(page_table, *([cache_t] * (ns * n_pages)), zrow, wnewt, wpast_t, kc, vc, cov, expand)


def _merge_kernel(x_ref, yp_ref, yn_ref, yc_ref, sg_ref, wbp_ref, wbn_ref, wbc_ref, wo_ref, o_ref):
    d = D_MODEL
    merged = (sg_ref[:, 0:d].astype(F32) * _dot(yp_ref[...], wbp_ref[...])
              + sg_ref[:, d:2 * d].astype(F32) * _dot(yn_ref[...], wbn_ref[...])
              + sg_ref[:, 2 * d:3 * d].astype(F32) * _dot(yc_ref[...], wbc_ref[...]))
    o_ref[...] = x_ref[...] + _dot(merged.astype(MXU), wo_ref[...])


def _merge(x, yp, yn, yc, sg, wbp, wbn, wbc, wo, *, tm):
    m = x.shape[0]
    row = lambda w: pl.BlockSpec((tm, w), lambda i: (i, 0))
    est = 2 * tm * (2 * D_MODEL * 4 + 1024 * 2 + 3072 * 2) + 2 * 3 * D_MODEL * D_MODEL + 6 * tm * D_MODEL * 4 + (2 << 20)
    return pl.pallas_call(
        _merge_kernel,
        grid=(m // tm,),
        in_specs=[row(D_MODEL), row(POOL_W), row(ATT_W), row(CONV_DIM), row(3 * D_MODEL),
                  _resident((POOL_W, D_MODEL)), _resident((ATT_W, D_MODEL)), _resident((CONV_DIM, D_MODEL)),
                  _resident((D_MODEL, D_MODEL))],
        out_specs=row(D_MODEL),
        out_shape=jax.ShapeDtypeStruct((m, D_MODEL), F32),
        compiler_params=_cparams(est),
        name="merge",
    )(x, yp, yn, yc, sg, wbp, wbn, wbc, wo)


def _ffn_kernel(x_ref, g_ref, pre_ref, wup_ref, wcv_ref, wdn_ref, gf_ref, o_ref, last_ref, *rest,
                tm, stride, tiles_per_seq, final):
    yf_ref = rest[0] if final else None
    aext = rest[-1]
    i = pl.program_id(0)
    hr = aext.shape[0] - tm

    @pl.when(i % tiles_per_seq == 0)
    def _():
        aext[0:hr, :] = pre_ref[...]

    x = x_ref[...]
    h = x * lax.rsqrt(jnp.mean(x * x, axis=-1, keepdims=True) + EPS)
    h = (h * g_ref[...]).astype(MXU)
    acc = x
    for j in range(D_FF // FF_CHUNK):
        cj = slice(j * FF_CHUNK, (j + 1) * FF_CHUNK)
        a = _dot(h, wup_ref[:, cj])
        aext[hr:hr + tm, cj] = a
        a1 = aext[hr - stride:hr - stride + tm, cj]
        a2 = aext[hr - 2 * stride:hr - 2 * stride + tm, cj]
        ac = wcv_ref[0:1, cj] * a2 + wcv_ref[1:2, cj] * a1 + wcv_ref[2:3, cj] * a
        bgate = _dot(h, wup_ref[:, D_FF + j * FF_CHUNK:D_FF + (j + 1) * FF_CHUNK])
        gated = (ac * _sigmoid(ac) * bgate).astype(MXU)
        acc = acc + _dot(gated, wdn_ref[cj, :])
    o_ref[...] = acc
    if final:
        yf = acc * lax.rsqrt(jnp.mean(acc * acc, axis=-1, keepdims=True) + EPS)
        yf_ref[...] = yf * gf_ref[...]
    keep = (CONV_TAPS - 1) * stride
    last_ref[0] = aext[hr + tm - keep:hr + tm, :]
    if tiles_per_seq > 1:
        aext[0:hr, :] = aext[tm:tm + hr, :]


def _conv_ffn(x, g, pre, wup, wcv, wdn, gf, *, tm, stride, tiles_per_seq, final):
    m = x.shape[0]
    nt = m // tm
    nseq = nt // tiles_per_seq
    hr = pre.shape[0] // nseq
    keep = (CONV_TAPS - 1) * stride
    row = pl.BlockSpec((tm, D_MODEL), lambda i: (i, 0))
    est = (3 * 2 * tm * D_MODEL * 4 + wup.size * 2 + wdn.size * 2 + (hr + tm) * D_FF * 4 + 2 * hr * D_FF * 4
           + 2 * keep * D_FF * 4 + 8 * tm * D_MODEL * 4 + (2 << 20))
    out_specs = [row, pl.BlockSpec((1, keep, D_FF), lambda i: (i // tiles_per_seq, 0, 0))]
    out_shape = [jax.ShapeDtypeStruct((m, D_MODEL), F32), jax.ShapeDtypeStruct((nseq, keep, D_FF), F32)]
    if final:
        out_specs.append(row)
        out_shape.append(jax.ShapeDtypeStruct((m, D_MODEL), F32))
    res = pl.pallas_call(
        functools.partial(_ffn_kernel, tm=tm, stride=stride, tiles_per_seq=tiles_per_seq, final=final),
        grid=(nt,),
        in_specs=[row, pl.BlockSpec((1, D_MODEL), lambda i: (0, 0)),
                  pl.BlockSpec((hr, D_FF), lambda i: (i // tiles_per_seq, 0)),
                  _resident((D_MODEL, 2 * D_FF)), pl.BlockSpec((CONV_TAPS, D_FF), lambda i: (0, 0)),
                  _resident((D_FF, D_MODEL)), pl.BlockSpec((1, D_MODEL), lambda i: (0, 0))],
        out_specs=tuple(out_specs),
        out_shape=tuple(out_shape),
        scratch_shapes=[pltpu.VMEM((hr + tm, D_FF), F32)],
        compiler_params=_cparams(est),
        name="conv_ffn",
    )(x, g, pre, wup, wcv, wdn, gf)
    return (res[0], res[1], res[2] if final else None)


def _rot_cols(w):
    k = w.shape[0]
    w4 = w.reshape(k, -1, 2, HEAD_DIM // 2)
    return jnp.stack([-w4[:, :, 1], w4[:, :, 0]], axis=2).reshape(k, -1)


def _rope_table_t(pos):
    half = HEAD_DIM // 2
    inv = ROPE_THETA ** (-jnp.arange(half, dtype=F32) / half)
    ang = pos.astype(F32)[:, None] * inv[None, :]
    return jnp.concatenate([jnp.tile(jnp.cos(ang), (1, 4)), jnp.tile(jnp.sin(ang), (1, 4))], axis=1).T


def _blockdiag2(w):
    z = jnp.zeros_like(w)
    return jnp.concatenate([jnp.concatenate([w, z], axis=1), jnp.concatenate([z, w], axis=1)], axis=0)


def _head_major_to_pair_major(w, axis):
    shp = w.shape
    w = w.reshape(shp[:axis] + (N_KV, Q_PER_KV, HEAD_DIM) + shp[axis + 1:])
    w = jnp.swapaxes(w, axis, axis + 1)
    return w.reshape(shp)


def _layer_weights(w_in, pool_w, cmp_pe, cmp_w1, cmp_w2, w_br_nsa):
    parts = []
    off = 0
    for w in IN_SPLITS:
        parts.append(w_in[:, off:off + w])
        off += w
    u, q, kv, gate, cb, cc, cx, gp, gn, gc = parts
    kv6 = kv.reshape(D_MODEL, 6, KV_W)
    kcmp, vcmp, ksel, vsel, kwin, vwin = [kv6[:, s] for s in range(6)]
    wrow = jnp.concatenate([u, cb, cc, cx, gp, gn, gc], axis=1).astype(MXU)
    qs = _head_major_to_pair_major(q, 1) * (HEAD_DIM ** -0.5 * LOG2E)
    gpad = jnp.pad(gate, ((0, 0), (0, LANES - gate.shape[1])))
    wt = jnp.concatenate([qs, _rot_cols(qs), kcmp, vcmp, ksel, _rot_cols(ksel), vsel,
                          kwin, _rot_cols(kwin), vwin, gpad], axis=1).T.astype(MXU)
    wbn = _head_major_to_pair_major(w_br_nsa, 0).astype(MXU)

    pw = jnp.zeros((POOL_W, POOL_W), F32)
    gsz = POOL_W // len(POOL_WINDOWS)
    for gi in range(len(POOL_WINDOWS)):
        pw = pw.at[gi * gsz:(gi + 1) * gsz, gi * gsz:(gi + 1) * gsz].set(pool_w[gi])

    wab = []
    peab = []
    for s in range(2):
        w1r = cmp_w1[s].reshape(CMP_LEN, HEAD_DIM, CMP_HIDDEN)
        halves = []
        for hsel in range(2):
            wh = w1r[hsel * CMP_STRIDE:(hsel + 1) * CMP_STRIDE]
            z = jnp.zeros((CMP_STRIDE, N_KV, HEAD_DIM, N_KV, CMP_HIDDEN), F32)
            for g in range(N_KV):
                z = z.at[:, g, :, g, :].set(wh)
            halves.append(z.reshape(CMP_STRIDE * KV_W, N_KV * CMP_HIDDEN))
            pe = cmp_pe[s][hsel * CMP_STRIDE:(hsel + 1) * CMP_STRIDE]
            peab.append(jnp.broadcast_to(pe[:, None, :], (CMP_STRIDE, N_KV, HEAD_DIM)).reshape(1, -1))
        wab.append(jnp.stack(halves))
    wab = jnp.stack(wab).astype(MXU)
    peab = jnp.concatenate(peab + [jnp.zeros((SUBLANES - 4, CMP_STRIDE * KV_W), F32)], axis=0)
    w2k = jnp.concatenate([_blockdiag2(cmp_w2[0]), _blockdiag2(_rot_cols(cmp_w2[0]))], axis=1).astype(MXU)
    w2v = _blockdiag2(cmp_w2[1]).astype(MXU)
    return dict(wrow=wrow, wt=wt, wbn=wbn, pw=pw.astype(MXU), wab=wab, peab=peab, w2k=w2k, w2v=w2v, w2vt=w2v.T)


def _cover(n_cmp, n_sel):
    c_start = jnp.arange(n_cmp) * CMP_STRIDE
    j_start = jnp.arange(n_sel) * SEL_BLOCK
    return ((c_start[:, None] < j_start[None, :] + SEL_BLOCK)
            & (c_start[:, None] + CMP_LEN > j_start[None, :])).astype(MXU)


def _feature_major(x):
    nd = x.ndim
    perm = tuple(range(nd - 4)) + (nd - 3, nd - 2, nd - 1, nd - 4)
    xt = jnp.transpose(x, perm)
    return xt.reshape(xt.shape[:nd - 4] + (-1, xt.shape[-1]))


def _token_major(xt, a, b, c):
    nd = xt.ndim
    x = xt.reshape(xt.shape[:nd - 2] + (a, b, c, xt.shape[-1]))
    perm = tuple(range(nd - 2)) + (nd + 1, nd - 2, nd - 1, nd)
    return jnp.transpose(x, perm)


def kernel(x_prompt, x_sample, cache_kv, cache_win, state_pool, state_conv, state_ffn, page_table,
           norm_mix, w_in, pool_w, pool_scale, cmp_pe, cmp_w1, cmp_w2, conv_w,
           w_br_pool, w_br_nsa, w_br_conv, w_out, norm_ffn, ffn_up, ffn_conv, ffn_down, norm_final):
    bp, sp, d = x_prompt.shape
    bd, sd, _ = x_sample.shape
    depth = w_in.shape[0]
    n_pages = page_table.shape[1]
    past = n_pages * PAGE
    wpast = cache_win.shape[2]
    assert d == D_MODEL and sp % (4 * Q_BLOCK) == 0 and sp >= WINDOW and sp // SEL_BLOCK <= LANES
    assert bd == LANES and sd & (sd - 1) == 0 and sd <= SUBLANES
    assert past % SEL_BLOCK == 0 and cache_kv.shape[2] == PAGE and wpast == WINDOW

    tm_p = 512
    mp = bp * sp
    md = bd * sd
    tps = sp // tm_p

    cst_p = _rope_table_t(jnp.arange(sp))
    cst_d = _rope_table_t(jnp.repeat(past + jnp.arange(sd), bd))
    nc_p = sp // CMP_STRIDE
    nc_d = past // CMP_STRIDE
    csc_p = _rope_table_t(jnp.arange(nc_p) * CMP_STRIDE + CMP_LEN - 1).T
    csc_d = _rope_table_t(jnp.arange(nc_d) * CMP_STRIDE + CMP_LEN - 1).T
    nsel_p = sp // SEL_BLOCK
    cov_pt = _cover(nc_p, nsel_p).T
    cov_d = _cover(nc_d, LANES)
    expand = (jnp.arange(LANES)[:, None] == (jnp.arange(past) // SEL_BLOCK)[None, :]).astype(MXU)
    onehot_p = ((jnp.arange(sp) // SEL_BLOCK)[:, None] == jnp.arange(LANES)[None, :]).astype(MXU)
    tok = jnp.arange(PAGE)
    perm = (tok[None, :] == (tok[:, None] % (PAGE // CMP_STRIDE)) * CMP_STRIDE + tok[:, None] // (PAGE // CMP_STRIDE))
    perm = perm.astype(MXU)

    hr_pool_p = 2 * SUBLANES
    hr_conv_p = SUBLANES
    zeros_pool = jnp.zeros((bp * hr_pool_p, POOL_W), F32)
    zeros_conv = jnp.zeros((bp * hr_conv_p, CONV_DIM), F32)
    zeros_ffn = jnp.zeros((bp * hr_conv_p, D_FF), F32)

    cache_t = _feature_major(cache_kv)
    win_t = _feature_major(cache_win)
    pool_tm = jnp.transpose(state_pool, (0, 2, 1, 3))

    xp = x_prompt.reshape(mp, d)
    xs = jnp.transpose(x_sample, (1, 0, 2)).reshape(md, d)

    keys = ("kv_p", "kv_s", "win_p", "win_s", "pool_p", "pool_s", "conv_p", "conv_s", "ffn_p", "ffn_s")
    outs = {k: [] for k in keys}
    yp_final = ys_final = None
    for l in range(depth):
        lw = _layer_weights(w_in[l], pool_w[l], cmp_pe[l], cmp_w1[l], cmp_w2[l], w_br_nsa[l])
        g_mix = norm_mix[l].reshape(1, d)
        g_ffn = norm_ffn[l].reshape(1, d)
        ps = pool_scale[l].reshape(1, POOL_W)
        cw = conv_w[l]
        wbp = w_br_pool[l].astype(MXU)
        wbc = w_br_conv[l].astype(MXU)
        wo = w_out[l].astype(MXU)
        wup = ffn_up[l].astype(MXU)
        wdn = ffn_down[l].astype(MXU)
        wcv = ffn_conv[l]
        gf = norm_final.reshape(1, d)
        final = l == depth - 1

        up, cb, ccx, sg, kvt, wint, qt, va, krows, gt = _in_proj(xp, g_mix, lw["wrow"], lw["wt"], cst_p,
                                                               tm=tm_p, seq=sp, sample=False)
        ypool, yconv = _mixers(up, cb, ccx, zeros_pool, zeros_conv, lw["pw"], ps, cw,
                               tm=tm_p, stride=1, tiles_per_seq=tps, pos0=0)
        kc, vct = _compress_prompt(kvt, perm, lw["peab"], lw["wab"], lw["w2k"], lw["w2vt"], csc_p)
        ynsa = _attention_prompt(qt, kc, vct, krows, va, gt, cov_pt, onehot_p, b=bp, s=sp)
        x1 = _merge(xp, ypool, ynsa, yconv, sg, wbp, lw["wbn"], wbc, wo, tm=tm_p)
        xp, a_last, yp_final = _conv_ffn(x1, g_ffn, zeros_ffn, wup, wcv, wdn, gf, tm=tm_p, stride=1,
                                         tiles_per_seq=tps, final=final)
        outs["kv_p"].append(kvt)
        outs["win_p"].append(wint[:, :, sp - WINDOW:])
        outs["pool_p"].append(up.reshape(bp, sp, POOL_W)[:, sp - (POOL_MAX - 1):])
        outs["conv_p"].append(ccx.reshape(bp, sp, CONV_DIM)[:, sp - (CONV_TAPS - 1):])
        outs["ffn_p"].append(a_last)

        up, cb, ccx, sg, zall = _in_proj(xs, g_mix, lw["wrow"], lw["wt"], cst_d, tm=md, seq=sd, sample=True)
        pool_pre = pool_tm[l].reshape((POOL_MAX - 1) * bd, POOL_W)
        conv_pre = jnp.transpose(state_conv[l], (1, 0, 2)).reshape((CONV_TAPS - 1) * bd, CONV_DIM)
        ffn_pre = jnp.transpose(state_ffn[l], (1, 0, 2)).reshape((CONV_TAPS - 1) * bd, D_FF)
        ypool, yconv = _mixers(up, cb, ccx, pool_pre, conv_pre, lw["pw"], ps, cw,
                               tm=md, stride=bd, tiles_per_seq=1, pos0=past)
        zrow = jnp.pad(jnp.transpose(zall, (2, 0, 1)), ((0, 0), (0, SUBLANES - sd), (0, 0)))
        wnew_t = zall[:, ATT_W + 4 * KV_W:ATT_W + 6 * KV_W, :]
        wnew_tp = jnp.pad(jnp.transpose(wnew_t, (2, 1, 0)), ((0, 0), (0, 0), (0, LANES - sd)))
        kc_d, vc_d = _compress_sample(page_table, cache_t, perm, lw["peab"], lw["wab"], lw["w2k"], lw["w2v"], csc_d,
                                      layer=l, nb=4)
        o_seq, win_new = _attention_sample(page_table, cache_t, zrow, wnew_tp, win_t, kc_d, vc_d, cov_d, expand,
                                           layer=l, ns=2, pos0=past, n_new=sd)
        ynsa = jnp.transpose(o_seq.reshape(bd, Q_PER_KV, SUBLANES, KV_W)[:, :, 0:sd], (2, 0, 1, 3))
        ynsa = ynsa.reshape(md, ATT_W).astype(MXU)
        x1 = _merge(xs, ypool, ynsa, yconv, sg, wbp, lw["wbn"], wbc, wo, tm=md)
        xs, a_last, ys_final = _conv_ffn(x1, g_ffn, ffn_pre, wup, wcv, wdn, gf, tm=md, stride=bd,
                                         tiles_per_seq=1, final=final)
        outs["kv_s"].append(zall[:, ATT_W:ATT_W + 4 * KV_W, :])
        outs["win_s"].append(win_new)
        up_tm = up.reshape(sd, bd, POOL_W)
        pool_full = jnp.concatenate([pool_tm[l], up_tm], axis=0)
        outs["pool_s"].append(pool_full[pool_full.shape[0] - (POOL_MAX - 1):])
        ccx_seq = jnp.transpose(ccx.reshape(sd, bd, CONV_DIM), (1, 0, 2))
        conv_full = jnp.concatenate([state_conv[l], ccx_seq], axis=1)
        outs["conv_s"].append(conv_full[:, conv_full.shape[1] - (CONV_TAPS - 1):])
        outs["ffn_s"].append(jnp.transpose(a_last.reshape(CONV_TAPS - 1, bd, D_FF), (1, 0, 2)))

    st = lambda k: jnp.stack(outs[k])
    y_prompt = yp_final.reshape(bp, sp, d)
    y_sample = jnp.transpose(ys_final.reshape(sd, bd, d), (1, 0, 2))
    kv_prompt = _token_major(st("kv_p"), 4, N_KV, HEAD_DIM)
    kv_s = st("kv_s").reshape(depth, sd, 4, N_KV, HEAD_DIM, bd)
    kv_sample = jnp.transpose(kv_s, (0, 5, 1, 2, 3, 4))
    win_prompt = _token_major(st("win_p"), 2, N_KV, HEAD_DIM)
    win_sample = _token_major(st("win_s"), 2, N_KV, HEAD_DIM)
    pool_sample = jnp.transpose(st("pool_s"), (0, 2, 1, 3))
    return (y_prompt, y_sample, kv_prompt, kv_sample, win_prompt, win_sample, st("pool_p"), pool_sample,
            st("conv_p"), st("conv_s"), st("ffn_p"), st("ffn_s"))
```

```python
import functools
import math

import jax
import jax.numpy as jnp
from jax import lax
from jax.experimental import pallas as pl
from jax.experimental.pallas import tpu as pltpu

D_MODEL = 1024
HEAD_DIM = 64
N_HEADS = 8
N_KV = 2
Q_PER_KV = 4
ATT_W = N_HEADS * HEAD_DIM
KV_W = N_KV * HEAD_DIM
CMP_LEN = 32
CMP_STRIDE = 16
CMP_HIDDEN = 128
SEL_BLOCK = 64
SEL_TOPK = 16
N_LOCAL = 2
WINDOW = 512
Q_BLOCK = 128
FORCED_SCORE = 1e4
POOL_WINDOWS = (2, 4, 8, 16)
POOL_W = 256
POOL_MAX = 16
CONV_TAPS = 3
CONV_DIM = 256
D_FF = 2816
ROPE_THETA = 10000.0
EPS = 1e-6
PAGE = 128
IN_SPLITS = (POOL_W, ATT_W, 6 * KV_W, 3 * N_HEADS, CONV_DIM, CONV_DIM, CONV_DIM, D_MODEL, D_MODEL, D_MODEL)

LANES = 128
SUBLANES = 8
BF16_ROWS = 16
V7X_VMEM_BYTES = 64 * 1024 * 1024
VMEM_CAP = 56 * 1024 * 1024

NEG = -1e30
MXU = jnp.bfloat16
F32 = jnp.float32
LOG2E = math.log2(math.e)

W_ROW = POOL_W + 3 * CONV_DIM + 3 * D_MODEL
T_Q = 2 * ATT_W
T_KV = 5 * KV_W
T_WIN = 3 * KV_W
W_T = T_Q + T_KV + T_WIN + LANES
VA_ROWS = HEAD_DIM + BF16_ROWS
Z_ALL = ATT_W + 4 * KV_W + 2 * KV_W + LANES
FF_CHUNK = 256


def _cparams(est_bytes, ndim=1):
    limit = int(min(max(est_bytes, 16 * 1024 * 1024), VMEM_CAP))
    return pltpu.CompilerParams(dimension_semantics=("arbitrary",) * ndim, vmem_limit_bytes=limit)


def _resident(shape):
    n = len(shape)
    return pl.BlockSpec(shape, lambda *_: (0,) * n, pipeline_mode=pl.Buffered(1))


def _sigmoid(x):
    return 1.0 / (1.0 + jnp.exp(-x))


def _dot(a, b):
    return jnp.dot(a, b, preferred_element_type=F32)


def _dot_nt(a, b):
    return lax.dot_general(a, b, (((1,), (1,)), ((), ())), preferred_element_type=F32)


def _split3(x):
    hi = x.astype(MXU)
    r1 = x - hi.astype(F32)
    mid = r1.astype(MXU)
    lo = (r1 - mid.astype(F32)).astype(MXU)
    return hi, mid, lo


def _dot3(x, w):
    out = None
    for term in _split3(x):
        t = _dot(term, w)
        out = t if out is None else out + t
    return out


def _inproj_kernel(x_ref, g_ref, wrow_ref, wt_ref, cst_ref, up_ref, cb_ref, ccx_ref, sg_ref, *outs, tm, sample):
    x = x_ref[...]
    h = x * lax.rsqrt(jnp.mean(x * x, axis=-1, keepdims=True) + EPS)
    h = (h * g_ref[...]).astype(MXU)

    up_ref[...] = _dot(h, wrow_ref[:, 0:POOL_W])
    z = _dot(h, wrow_ref[:, POOL_W:POOL_W + 3 * CONV_DIM])
    cb_ref[...] = z[:, 0:256]
    ccx_ref[...] = z[:, 256:512] * z[:, 512:768]
    g0 = POOL_W + 3 * CONV_DIM
    for j in range(6):
        z = _dot(h, wrow_ref[:, g0 + 512 * j:g0 + 512 * (j + 1)])
        sg_ref[:, 512 * j:512 * (j + 1)] = _sigmoid(z).astype(sg_ref.dtype)

    cost = cst_ref[0:LANES, :]
    sint = cst_ref[LANES:2 * LANES, :]
    zq = _dot_nt(wt_ref[0:T_Q, :], h)
    q = jnp.concatenate(
        [zq[i * LANES:(i + 1) * LANES] * cost + zq[ATT_W + i * LANES:ATT_W + (i + 1) * LANES] * sint
         for i in range(ATT_W // LANES)], axis=0)
    zk = _dot_nt(wt_ref[T_Q:T_Q + T_KV, :], h)
    ksel = zk[256:384] * cost + zk[384:512] * sint
    kv = jnp.concatenate([zk[0:256], ksel, zk[512:640]], axis=0)
    zw = _dot_nt(wt_ref[T_Q + T_KV:T_Q + T_KV + T_WIN, :], h)
    kwin = zw[0:128] * cost + zw[128:256] * sint
    win = jnp.concatenate([kwin, zw[256:384]], axis=0)
    gates = _sigmoid(_dot_nt(wt_ref[T_Q + T_KV + T_WIN:W_T, :], h))

    if sample:
        (zall_ref,) = outs
        for c in range(tm // LANES):
            cols = slice(c * LANES, (c + 1) * LANES)
            zall_ref[c] = jnp.concatenate([q[:, cols], kv[:, cols], win[:, cols], gates[:, cols]], axis=0)
    else:
        kvt_ref, wint_ref, qt_ref, va_ref, krows_ref, gt_ref = outs
        kvt_ref[0] = kv
        wint_ref[0] = win
        ones = jnp.ones((BF16_ROWS, LANES), va_ref.dtype)
        vsel = kv[384:512].astype(va_ref.dtype)
        vwin = win[128:256].astype(va_ref.dtype)
        for c in range(tm // LANES):
            cols = slice(c * LANES, (c + 1) * LANES)
            qt_ref[c] = q[:, cols].astype(qt_ref.dtype)
            va_ref[c] = jnp.concatenate(
                [vsel[0:64, cols], ones, vsel[64:128, cols], ones, vwin[0:64, cols], ones, vwin[64:128, cols], ones],
                axis=0)
            gt_ref[c] = gates[:, cols]
        krows_ref[:, 0:KV_W] = ksel.T.astype(krows_ref.dtype)
        krows_ref[:, KV_W:2 * KV_W] = kwin.T.astype(krows_ref.dtype)


def _in_proj(x, g, wrow, wt, cst, *, tm, seq, sample):
    m = x.shape[0]
    nt = m // tm
    ntab = cst.shape[1] // tm
    nb = tm // LANES
    tps = seq // tm if not sample else 1
    row = lambda w: pl.BlockSpec((tm, w), lambda i: (i, 0))
    blk = lambda r: pl.BlockSpec((nb, r, LANES), lambda i: (i, 0, 0))
    out_shape = [jax.ShapeDtypeStruct((m, POOL_W), F32), jax.ShapeDtypeStruct((m, CONV_DIM), F32),
                 jax.ShapeDtypeStruct((m, CONV_DIM), F32), jax.ShapeDtypeStruct((m, 3 * D_MODEL), MXU)]
    out_specs = [row(POOL_W), row(CONV_DIM), row(CONV_DIM), row(3 * D_MODEL)]
    if sample:
        out_shape += [jax.ShapeDtypeStruct((m // LANES, Z_ALL, LANES), F32)]
        out_specs += [blk(Z_ALL)]
    else:
        out_shape += [jax.ShapeDtypeStruct((m // seq, 4 * KV_W, seq), F32),
                      jax.ShapeDtypeStruct((m // seq, 2 * KV_W, seq), F32),
                      jax.ShapeDtypeStruct((m // LANES, ATT_W, LANES), MXU),
                      jax.ShapeDtypeStruct((m // LANES, 4 * VA_ROWS, LANES), MXU),
                      jax.ShapeDtypeStruct((m, 2 * KV_W), MXU),
                      jax.ShapeDtypeStruct((m // LANES, LANES, LANES), F32)]
        out_specs += [pl.BlockSpec((1, 4 * KV_W, tm), lambda i: (i // tps, 0, i % tps)),
                      pl.BlockSpec((1, 2 * KV_W, tm), lambda i: (i // tps, 0, i % tps)),
                      blk(ATT_W), blk(4 * VA_ROWS), row(2 * KV_W), blk(LANES)]
    est = (2 * tm * D_MODEL * 4 + wrow.size * 2 + wt.size * 2 + 4 * tm * 256 * 4
           + 2 * tm * (3 * 256 * 4 + 3072 * 2) + 2 * tm * Z_ALL * 4 + tm * (T_Q + T_KV + T_WIN + 1024 + 1024) * 4
           + (4 << 20))
    return pl.pallas_call(
        functools.partial(_inproj_kernel, tm=tm, sample=sample),
        grid=(nt,),
        in_specs=[pl.BlockSpec((tm, D_MODEL), lambda i: (i, 0)),
                  pl.BlockSpec((1, D_MODEL), lambda i: (0, 0)),
                  _resident((D_MODEL, W_ROW)),
                  _resident((W_T, D_MODEL)),
                  pl.BlockSpec((2 * LANES, tm), lambda i: (0, i % ntab))],
        out_specs=tuple(out_specs),
        out_shape=tuple(out_shape),
        compiler_params=_cparams(est),
        name="in_proj",
    )(x, g, wrow, wt, cst)


def _mix_kernel(up_ref, cb_ref, ccx_ref, ppre_ref, cpre_ref, pw_ref, ps_ref, cw_ref,
                yp_ref, yc_ref, pext, cext, *, tm, stride, tiles_per_seq, pos0):
    i = pl.program_id(0)
    hp = pext.shape[0] - tm
    hc = cext.shape[0] - tm

    @pl.when(i % tiles_per_seq == 0)
    def _():
        pext[0:hp, :] = ppre_ref[...]
        cext[0:hc, :] = cpre_ref[...]

    u = up_ref[...]
    pext[hp:hp + tm, :] = u
    acc = u
    sums = {}
    for k in range(1, POOL_MAX):
        acc = acc + pext[hp - k * stride:hp - k * stride + tm, :]
        if k + 1 in POOL_WINDOWS:
            sums[k + 1] = acc
    row = lax.broadcasted_iota(jnp.int32, (tm, 1), 0)
    if stride == 1:
        t_abs = pos0 + (i % tiles_per_seq) * tm + row
    else:
        t_abs = pos0 + lax.shift_right_logical(row, stride.bit_length() - 1)
    tp1 = (t_abs + 1).astype(F32)
    lane = lax.broadcasted_iota(jnp.int32, (1, POOL_W), 1)
    grp = POOL_W // len(POOL_WINDOWS)
    mean = None
    for gi, w in reversed(list(enumerate(POOL_WINDOWS))):
        mw = sums[w] / jnp.minimum(float(w), tp1)
        mean = mw if mean is None else jnp.where(lane < (gi + 1) * grp, mw, mean)
    pooled = mean - u
    y = _dot(pooled.astype(MXU), pw_ref[...]) * ps_ref[...]
    yp_ref[...] = y.astype(yp_ref.dtype)

    e0 = ccx_ref[...]
    cext[hc:hc + tm, :] = e0
    e1 = cext[hc - stride:hc - stride + tm, :]
    e2 = cext[hc - 2 * stride:hc - 2 * stride + tm, :]
    conv = cw_ref[0:1, :] * e2 + cw_ref[1:2, :] * e1 + cw_ref[2:3, :] * e0
    yc_ref[...] = (cb_ref[...] * conv).astype(yc_ref.dtype)

    if tiles_per_seq > 1:
        pext[0:hp, :] = pext[tm:tm + hp, :]
        cext[0:hc, :] = cext[tm:tm + hc, :]


def _mixers(up, cb, ccx, ppre, cpre, pw, ps, cw, *, tm, stride, tiles_per_seq, pos0):
    m = up.shape[0]
    nt = m // tm
    hp = ppre.shape[0] // (nt // tiles_per_seq)
    hc = cpre.shape[0] // (nt // tiles_per_seq)
    row = pl.BlockSpec((tm, 256), lambda i: (i, 0))
    est = 2 * 5 * tm * 256 * 4 + (2 * tm + hp + hc) * 256 * 4 + 2 * (hp + hc) * 256 * 4 + 12 * tm * 256 * 4 + (2 << 20)
    return pl.pallas_call(
        functools.partial(_mix_kernel, tm=tm, stride=stride, tiles_per_seq=tiles_per_seq, pos0=pos0),
        grid=(nt,),
        in_specs=[row, row, row,
                  pl.BlockSpec((hp, 256), lambda i: (i // tiles_per_seq, 0)),
                  pl.BlockSpec((hc, 256), lambda i: (i // tiles_per_seq, 0)),
                  pl.BlockSpec((256, 256), lambda i: (0, 0)),
                  pl.BlockSpec((1, 256), lambda i: (0, 0)),
                  pl.BlockSpec((CONV_TAPS, 256), lambda i: (0, 0))],
        out_specs=(row, row),
        out_shape=(jax.ShapeDtypeStruct((m, 256), MXU), jax.ShapeDtypeStruct((m, 256), MXU)),
        scratch_shapes=[pltpu.VMEM((hp + tm, 256), F32), pltpu.VMEM((hc + tm, 256), F32)],
        compiler_params=_cparams(est),
        name="mixers",
    )(up, cb, ccx, ppre, cpre, pw, ps, cw)


def _gelu_tanh(x):
    return x * (0.5 * (1.0 + jnp.tanh(0.7978845608028654 * (x + 0.044715 * (x * x * x)))))


def _compress_bias(peab_ref, wab_ref):
    pe3 = _split3(peab_ref[...])
    out = []
    for s in range(2):
        ba = None
        bb = None
        for term in pe3:
            ta = _dot(term, wab_ref[s, 0])
            tb = _dot(term, wab_ref[s, 1])
            ba = ta if ba is None else ba + ta
            bb = tb if bb is None else bb + tb
        out.append(ba[2 * s:2 * s + 1, :] + bb[2 * s + 1:2 * s + 2, :])
    return out


def _compress_hidden(get_page, n_pages, perm_ref, wab_ref, bias, pb_scr):
    n = n_pages * (PAGE // CMP_STRIDE)
    per = PAGE // CMP_STRIDE
    ys = ([], [])
    for p in range(n_pages):
        yp = _dot_nt(perm_ref[...], get_page(p).astype(MXU))
        for s in range(2):
            ys[s].append(jnp.concatenate(
                [yp[t * per:(t + 1) * per, s * KV_W:(s + 1) * KV_W] for t in range(CMP_STRIDE)], axis=1))
    acts = []
    for s in range(2):
        rows = jnp.concatenate(ys[s], axis=0).astype(MXU)
        pa = _dot(rows, wab_ref[s, 0])
        pb = _dot(rows, wab_ref[s, 1])
        pb_scr[0:n, :] = pb
        pb_scr[n:n + SUBLANES, :] = jnp.zeros((SUBLANES, 2 * CMP_HIDDEN), F32)
        pre = pa + pb_scr[1:n + 1, :] + bias[s]
        acts.append(_gelu_tanh(pre).astype(MXU))
    return acts


def _cmp_prompt_kernel(kvt_ref, perm_ref, peab_ref, wab_ref, w2k_ref, w2vt_ref, csc_ref, kc_ref, vct_ref, pb_scr,
                       *, n_pages):
    def get_page(p):
        return kvt_ref[0, :, p * PAGE:(p + 1) * PAGE]

    bias = _compress_bias(peab_ref, wab_ref)
    act_k, act_v = _compress_hidden(get_page, n_pages, perm_ref, wab_ref, bias, pb_scr)
    yk = _dot(act_k, w2k_ref[...])
    kc = yk[:, 0:LANES] * csc_ref[:, 0:LANES] + yk[:, LANES:2 * LANES] * csc_ref[:, LANES:2 * LANES]
    kc_ref[0] = kc.astype(kc_ref.dtype)
    vct_ref[0] = _dot_nt(w2vt_ref[...], act_v).astype(vct_ref.dtype)


def _cmp_sample_kernel(pt_ref, *refs, n_pages, nb):
    del pt_ref
    page_refs = refs[0:nb * n_pages]
    perm_ref, peab_ref, wab_ref, w2k_ref, w2v_ref, csc_ref, kc_ref, vc_ref, pb_scr, bias_scr = refs[nb * n_pages:]
    n = n_pages * (PAGE // CMP_STRIDE)

    @pl.when(pl.program_id(0) == 0)
    def _():
        for s, b in enumerate(_compress_bias(peab_ref, wab_ref)):
            bias_scr[s] = jnp.broadcast_to(b, (SUBLANES, 2 * CMP_HIDDEN))

    bias = [bias_scr[s, 0:1, :] for s in range(2)]
    act_k, act_v = _compress_hidden(lambda p: page_refs[p][0, 0], nb * n_pages, perm_ref, wab_ref, bias, pb_scr)
    yk = _dot(act_k, w2k_ref[...])
    vc = _dot(act_v, w2v_ref[...])
    for j in range(nb):
        r = slice(j * n, (j + 1) * n)
        kc = yk[r, 0:LANES] * csc_ref[:, 0:LANES] + yk[r, LANES:2 * LANES] * csc_ref[:, LANES:2 * LANES]
        kc_ref[j] = kc.astype(kc_ref.dtype)
        vc_ref[j] = vc[r].astype(vc_ref.dtype)


def _compress_sample(page_table, cache_t, perm, peab, wab, w2k, w2v, csc, *, layer, nb):
    bd, n_pages = page_table.shape
    n = n_pages * (PAGE // CMP_STRIDE)
    in_specs = ([pl.BlockSpec((1, 1, 2 * KV_W, PAGE), lambda i, pt, j=j, k=k: (layer, pt[i * nb + j, k], 0, 0))
                 for j in range(nb) for k in range(n_pages)]
                + [pl.BlockSpec((PAGE, PAGE), lambda i, pt: (0, 0)),
                   pl.BlockSpec((SUBLANES, 2048), lambda i, pt: (0, 0)),
                   pl.BlockSpec((2, 2, 2048, 256), lambda i, pt: (0, 0, 0, 0), pipeline_mode=pl.Buffered(1)),
                   pl.BlockSpec((256, 256), lambda i, pt: (0, 0)),
                   pl.BlockSpec((256, 128), lambda i, pt: (0, 0)),
                   pl.BlockSpec((n, 256), lambda i, pt: (0, 0))])
    out = pl.BlockSpec((nb, n, LANES), lambda i, pt: (i, 0, 0))
    est = 2 * nb * n_pages * 2 * KV_W * PAGE * 4 + wab.size * 2 + 10 * nb * n * 2048 * 4 + (4 << 20)
    return pl.pallas_call(
        functools.partial(_cmp_sample_kernel, n_pages=n_pages, nb=nb),
        grid_spec=pltpu.PrefetchScalarGridSpec(
            num_scalar_prefetch=1,
            grid=(bd // nb,),
            in_specs=in_specs,
            out_specs=(out, out),
            scratch_shapes=[pltpu.VMEM((nb * n + SUBLANES, 2 * CMP_HIDDEN), F32),
                            pltpu.VMEM((2, SUBLANES, 2 * CMP_HIDDEN), F32)]),
        out_shape=(jax.ShapeDtypeStruct((bd, n, LANES), MXU), jax.ShapeDtypeStruct((bd, n, LANES), MXU)),
        compiler_params=_cparams(est),
        name="compress_sample",
    )(page_table, *([cache_t] * (nb * n_pages)), perm, peab, wab, w2k, w2v, csc)


def _compress_prompt(kvt, perm, peab, wab, w2k, w2vt, csc):
    b, _, s = kvt.shape
    n_pages = s // PAGE
    n = s // CMP_STRIDE
    est = 2 * 256 * s * 4 + wab.size * 2 + 8 * n * 2048 * 4 + (4 << 20)
    return pl.pallas_call(
        functools.partial(_cmp_prompt_kernel, n_pages=n_pages),
        grid=(b,),
        in_specs=[pl.BlockSpec((1, 2 * KV_W, s), lambda i: (i, 0, 0)),
                  pl.BlockSpec((PAGE, PAGE), lambda i: (0, 0)),
                  pl.BlockSpec((SUBLANES, 2048), lambda i: (0, 0)),
                  _resident((2, 2, 2048, 256)),
                  pl.BlockSpec((256, 256), lambda i: (0, 0)),
                  pl.BlockSpec((128, 256), lambda i: (0, 0)),
                  pl.BlockSpec((n, 256), lambda i: (0, 0))],
        out_specs=(pl.BlockSpec((1, n, LANES), lambda i: (i, 0, 0)),
                   pl.BlockSpec((1, LANES, n), lambda i: (i, 0, 0))),
        out_shape=(jax.ShapeDtypeStruct((b, n, LANES), MXU), jax.ShapeDtypeStruct((b, LANES, n), MXU)),
        scratch_shapes=[pltpu.VMEM((n + SUBLANES, 2 * CMP_HIDDEN), F32)],
        compiler_params=_cparams(est),
        name="compress_prompt",
    )(kvt, perm, peab, wab, w2k, w2vt, csc)


def _attn_prompt_kernel(qt_ref, kc_ref, vct_ref, krows_ref, va_ref, gt_ref, cov_ref, oh_ref,
                        y_ref, qaug, score_scr, *, n_cmp, n_sel):
    qb = pl.program_id(1)
    q0 = qb * Q_BLOCK
    nq = N_HEADS * Q_BLOCK
    half = Q_PER_KV * Q_BLOCK

    qt = qt_ref[0]
    rsel = lax.broadcasted_iota(jnp.int32, (2 * HEAD_DIM, 1), 0) < HEAD_DIM
    zero = jnp.zeros((), qt.dtype)
    for g in range(N_KV):
        for r in range(Q_PER_KV):
            blk = qt[r * LANES:(r + 1) * LANES, :]
            keep = rsel if g == 0 else jnp.logical_not(rsel)
            qaug[0:LANES, (g * Q_PER_KV + r) * Q_BLOCK:(g * Q_PER_KV + r + 1) * Q_BLOCK] = jnp.where(keep, blk, zero)
    qbd = qaug[0:LANES, :]

    lane = lax.broadcasted_iota(jnp.int32, (1, nq), 1)
    qpos = q0 + jnp.bitwise_and(lane, Q_BLOCK - 1)

    sc = _dot(kc_ref[0], qbd)
    posc = lax.broadcasted_iota(jnp.int32, (n_cmp, 1), 0) * CMP_STRIDE + (CMP_LEN - 1)
    valid = posc <= qpos
    s = jnp.where(valid, sc, NEG)
    mx = jnp.max(s, axis=0, keepdims=True)
    mx = jnp.where(mx > 0.5 * NEG, mx, 0.0)
    e = jnp.where(valid, jnp.exp2(s - mx), 0.0)
    p = e / jnp.maximum(jnp.sum(e, axis=0, keepdims=True), 1.0)
    pm = p.astype(MXU)
    vct = vct_ref[0]
    o_cmp = [_dot(vct[g * HEAD_DIM:(g + 1) * HEAD_DIM, :], pm[:, g * half:(g + 1) * half]) for g in range(N_KV)]

    w2 = N_KV * Q_BLOCK
    jrow = lax.broadcasted_iota(jnp.int32, (n_sel, 1), 0)
    qp2 = q0 + jnp.bitwise_and(lax.broadcasted_iota(jnp.int32, (1, w2), 1), Q_BLOCK - 1)
    cur = lax.shift_right_logical(qp2, 6)
    forced = (jrow == 0) | ((jrow <= cur) & (jrow > cur - N_LOCAL))
    causal = jrow * SEL_BLOCK <= qp2
    psum = []
    for g in range(N_KV):
        ps = p[:, g * half:g * half + Q_BLOCK]
        for r in range(1, Q_PER_KV):
            ps = ps + p[:, g * half + r * Q_BLOCK:g * half + (r + 1) * Q_BLOCK]
        psum.append(ps)
    imp = None
    for term in _split3(jnp.concatenate(psum, axis=1)):
        t = _dot(cov_ref[...], term)
        imp = t if imp is None else imp + t
    score = jnp.where(forced, FORCED_SCORE, jnp.where(causal, imp, -1.0))
    score_scr[...] = score
    n_live = jnp.minimum(n_sel, 2 * qb + 2)

    rank_unroll = 4

    def rank_body(i, cnt):
        for u in range(rank_unroll):
            jp = i * rank_unroll + u
            rowv = score_scr[pl.ds(jp, 1), :]
            ahead = (rowv > score) | ((rowv == score) & (jp < jrow))
            cnt = cnt + jnp.where(ahead, 1.0, 0.0)
        return cnt

    cnt = lax.fori_loop(0, (n_live + rank_unroll - 1) // rank_unroll, rank_body, jnp.zeros((n_sel, w2), F32))
    bias = jnp.where((cnt < float(SEL_TOPK)) & causal, 0.0, NEG).astype(MXU)
    qaug[LANES:2 * LANES, :] = jnp.zeros((LANES, nq), MXU)
    qaug[LANES:LANES + n_sel, :] = jnp.concatenate(
        [bias[:, 0:Q_BLOCK]] * Q_PER_KV + [bias[:, Q_BLOCK:w2]] * Q_PER_KV, axis=1)

    def step(state, k0, kk, rhs, vrow0, visible):
        m, a0, a1 = state
        nk = kk.shape[0]
        sij = _dot(kk, rhs)
        if visible is not None:
            kp = k0 + lax.broadcasted_iota(jnp.int32, (nk, 1), 0)
            sij = jnp.where(visible(kp), sij, NEG)
        m_new = jnp.maximum(m, jnp.max(sij, axis=0, keepdims=True))
        alpha = jnp.exp2(m - m_new)
        pij = jnp.exp2(sij - m_new).astype(MXU)
        vblk = jnp.concatenate([va_ref[k0 // LANES + sub][vrow0:vrow0 + 2 * VA_ROWS, :]
                                for sub in range(nk // LANES)], axis=1)
        a0 = alpha[:, 0:half] * a0 + _dot(vblk[0:VA_ROWS], pij[:, 0:half])
        a1 = alpha[:, half:nq] * a1 + _dot(vblk[VA_ROWS:2 * VA_ROWS], pij[:, half:nq])
        return m_new, a0, a1

    def merge(sa, sb):
        ma, a0a, a1a = sa
        mb, a0b, a1b = sb
        mm = jnp.maximum(ma, mb)
        wa = jnp.exp2(ma - mm)
        wb = jnp.exp2(mb - mm)
        a0 = wa[:, 0:half] * a0a + wb[:, 0:half] * a0b
        a1 = wa[:, half:nq] * a1a + wb[:, half:nq] * a1b
        return [a[0:HEAD_DIM] * (1.0 / a[HEAD_DIM:HEAD_DIM + 1]) for a in (a0, a1)]

    init = (jnp.full((1, nq), NEG, F32), jnp.zeros((VA_ROWS, half), F32), jnp.zeros((VA_ROWS, half), F32))

    sc_keys = 2 * Q_BLOCK

    def sel_step(state, c, causal_mask):
        k0 = pl.multiple_of(c * sc_keys, sc_keys)
        kk = jnp.concatenate([krows_ref[pl.ds(k0, sc_keys), 0:KV_W], oh_ref[pl.ds(k0, sc_keys), :]], axis=1)
        return step(state, k0, kk, qaug[...], 0, (lambda kp: kp <= qpos) if causal_mask else None)

    n_pairs = qb // 4
    sa, sb = lax.fori_loop(0, n_pairs,
                           lambda i, st: (sel_step(st[0], 2 * i, False), sel_step(st[1], 2 * i + 1, False)),
                           (init, init))
    sel_o = merge(sel_step(sa, 2 * n_pairs, True), sel_step(sb, 2 * n_pairs + 1, True))

    n_back = WINDOW // Q_BLOCK

    def win_step(state, c):
        blk = qb - n_back + c
        ok = blk >= 0
        k0 = pl.multiple_of(jnp.maximum(blk, 0) * Q_BLOCK, Q_BLOCK)
        kk = krows_ref[pl.ds(k0, Q_BLOCK), KV_W:2 * KV_W]
        return step(state, k0, kk, qbd, 2 * VA_ROWS, lambda kp: (kp <= qpos) & (kp >= qpos - WINDOW) & ok)

    sa = init
    sb = init
    for c in range(n_back + 1):
        if c % 2 == 0:
            sa = win_step(sa, c)
        else:
            sb = win_step(sb, c)
    win_o = merge(sa, sb)

    gt = gt_ref[0]
    for r in range(Q_PER_KV):
        c = slice(r * Q_BLOCK, (r + 1) * Q_BLOCK)
        pair = []
        for g in range(N_KV):
            hh = g * Q_PER_KV + r
            pair.append(o_cmp[g][:, c] * gt[hh:hh + 1, :] + sel_o[g][:, c] * gt[N_HEADS + hh:N_HEADS + hh + 1, :]
                        + win_o[g][:, c] * gt[2 * N_HEADS + hh:2 * N_HEADS + hh + 1, :])
        y_ref[:, r * LANES:(r + 1) * LANES] = jnp.concatenate(pair, axis=0).T.astype(y_ref.dtype)


def _attention_prompt(qt, kc, vct, krows, va, gt, cov, onehot, *, b, s):
    nqb = s // Q_BLOCK
    n_cmp = kc.shape[1]
    n_sel = s // SEL_BLOCK
    est = (2 * s * 2 * KV_W * 2 + 2 * (s // LANES) * 4 * VA_ROWS * LANES * 2 + s * LANES * 2 + 10 * 256 * 1024 * 4
           + 5 * n_cmp * 1024 * 4 + (6 << 20))
    return pl.pallas_call(
        functools.partial(_attn_prompt_kernel, n_cmp=n_cmp, n_sel=n_sel),
        grid=(b, nqb),
        in_specs=[pl.BlockSpec((1, ATT_W, LANES), lambda i, j: (i * nqb + j, 0, 0)),
                  pl.BlockSpec((1, n_cmp, LANES), lambda i, j: (i, 0, 0)),
                  pl.BlockSpec((1, LANES, n_cmp), lambda i, j: (i, 0, 0)),
                  pl.BlockSpec((s, 2 * KV_W), lambda i, j: (i, 0)),
                  pl.BlockSpec((s // LANES, 4 * VA_ROWS, LANES), lambda i, j: (i, 0, 0)),
                  pl.BlockSpec((1, LANES, LANES), lambda i, j: (i * nqb + j, 0, 0)),
                  pl.BlockSpec((n_sel, n_cmp), lambda i, j: (0, 0)),
                  _resident((s, LANES))],
        out_specs=pl.BlockSpec((Q_BLOCK, ATT_W), lambda i, j: (i * nqb + j, 0)),
        out_shape=jax.ShapeDtypeStruct((b * s, ATT_W), MXU),
        scratch_shapes=[pltpu.VMEM((2 * LANES, N_HEADS * Q_BLOCK), MXU), pltpu.VMEM((n_sel, N_KV * Q_BLOCK), F32)],
        compiler_params=_cparams(est, 2),
        name="attention_prompt",
    )(qt, kc, vct, krows, va, gt, cov, onehot)


def _attn_sample_kernel(pt_ref, *refs, n_pages, ns, n_new, pos0, n_sel, win_past):
    del pt_ref
    page_refs = [refs[j * n_pages:(j + 1) * n_pages] for j in range(ns)]
    zrow_ref, wnewt_ref, wpast_ref, kc_ref, vc_ref, cov_ref, exp_ref, o_ref, wout_ref = refs[ns * n_pages:]
    n = n_pages * (PAGE // CMP_STRIDE)
    past = n_pages * PAGE
    grp = Q_PER_KV * SUBLANES
    per_seq = N_KV * grp
    rows = ns * per_seq
    g_off = ATT_W + 6 * KV_W

    lane_h = lax.broadcasted_iota(jnp.int32, (1, KV_W), 1)
    head_lanes = [(lane_h >= g * HEAD_DIM) & (lane_h < (g + 1) * HEAD_DIM) for g in range(N_KV)]
    zrows = [zrow_ref[j] for j in range(ns)]

    def per_row(fn):
        return jnp.concatenate([fn(j, g, r) for j in range(ns) for g in range(N_KV) for r in range(Q_PER_KV)], axis=0)

    def per_seq_rows(fn):
        return jnp.concatenate([jnp.broadcast_to(fn(j), (per_seq, fn(j).shape[1])) for j in range(ns)], axis=0)

    def seq_matmul(fn):
        return jnp.concatenate([fn(j, slice(j * per_seq, (j + 1) * per_seq)) for j in range(ns)], axis=0)

    qf = per_row(lambda j, g, r: jnp.where(head_lanes[g], zrows[j][:, r * LANES:(r + 1) * LANES], 0.0))
    q = qf.astype(MXU)
    gate = [per_row(lambda j, g, r, k=k: zrows[j][:, g_off + k * N_HEADS + g * Q_PER_KV + r:
                                                  g_off + k * N_HEADS + g * Q_PER_KV + r + 1]) for k in range(3)]

    step = jnp.bitwise_and(lax.broadcasted_iota(jnp.int32, (rows, 1), 0), SUBLANES - 1)
    qp = pos0 + step
    lane_c = lax.broadcasted_iota(jnp.int32, (1, n), 1)
    valid_c = lane_c * CMP_STRIDE + (CMP_LEN - 1) <= qp
    wl = lax.broadcasted_iota(jnp.int32, (1, win_past), 1)
    wpos = pos0 - win_past + wl
    valid_w = (wpos >= 0) & (wpos <= qp) & (wpos >= qp - WINDOW)

    sc = seq_matmul(lambda j, r: _dot_nt(q[r], kc_ref[j]))
    s = jnp.where(valid_c, sc, NEG)
    mx = jnp.max(s, axis=1, keepdims=True)
    mx = jnp.where(mx > 0.5 * NEG, mx, 0.0)
    e = jnp.where(valid_c, jnp.exp2(s - mx), 0.0)
    p = e / jnp.maximum(jnp.sum(e, axis=1, keepdims=True), 1.0)
    pm = p.astype(MXU)
    o_c = seq_matmul(lambda j, r: _dot(pm[r], vc_ref[j]))

    n_grp = ns * N_KV
    psum = jnp.concatenate([sum(p[gi * grp + r * SUBLANES:gi * grp + (r + 1) * SUBLANES] for r in range(Q_PER_KV))
                            for gi in range(n_grp)], axis=0)
    imp = _dot3(psum, cov_ref[...])
    jl = lax.broadcasted_iota(jnp.int32, (1, LANES), 1)
    qp8 = pos0 + jnp.bitwise_and(lax.broadcasted_iota(jnp.int32, (n_grp * SUBLANES, 1), 0), SUBLANES - 1)
    cur = lax.shift_right_logical(qp8, 6)
    forced = (jl == 0) | ((jl <= cur) & (jl > cur - N_LOCAL))
    causal = (jl * SEL_BLOCK <= qp8) & (jl < n_sel)
    score = jnp.where(forced, FORCED_SCORE, jnp.where(causal, imp, -1.0))
    cnt = jnp.zeros_like(score)
    for jp in range(n_sel):
        col = score[:, jp:jp + 1]
        cnt = cnt + jnp.where((col > score) | ((col == score) & (jp < jl)), 1.0, 0.0)
    sel8 = ((cnt < float(SEL_TOPK)) & causal).astype(F32)
    sel = jnp.concatenate([sel8[gi * SUBLANES:(gi + 1) * SUBLANES] for gi in range(n_grp) for _ in range(Q_PER_KV)],
                          axis=0)
    keymask = _dot(sel.astype(MXU), exp_ref[...])

    def new_keys(lo, ok_fn):
        out = []
        for tk in range(n_new):
            krow = per_seq_rows(lambda j: zrows[j][tk:tk + 1, lo:lo + KV_W])
            vrow = per_seq_rows(lambda j: zrows[j][tk:tk + 1, lo + KV_W:lo + 2 * KV_W])
            sv = jnp.sum(qf * krow, axis=1, keepdims=True)
            out.append((jnp.where(ok_fn(tk), sv, NEG), vrow))
        return out

    def softmax_two(s_past, extra, v_of):
        mx = jnp.max(s_past, axis=1, keepdims=True)
        for sv, _ in extra:
            mx = jnp.maximum(mx, sv)
        pp = jnp.exp2(s_past - mx)
        den = jnp.sum(pp, axis=1, keepdims=True)
        pb = pp.astype(MXU)
        acc = seq_matmul(lambda j, r: _dot_nt(pb[r], v_of(j)))
        for sv, vv in extra:
            pe = jnp.exp2(sv - mx)
            den = den + pe
            acc = acc + pe * vv
        return acc / den

    def sel_rows(j, lo):
        return jnp.concatenate([pr[0, 0, lo:lo + KV_W, :] for pr in page_refs[j]], axis=1).astype(MXU)

    s_sel = jnp.where(keymask > 0.5, seq_matmul(lambda j, r: _dot(q[r], sel_rows(j, 0))), NEG)
    new_blk = past // SEL_BLOCK
    sel_new = sel[:, new_blk:new_blk + 1] > 0.5
    extra = new_keys(ATT_W + 2 * KV_W, lambda tk: sel_new & (tk <= step))
    o_s = softmax_two(s_sel, extra, lambda j: sel_rows(j, KV_W))

    wpast = [wpast_ref[0, j] for j in range(ns)]
    s_win = jnp.where(valid_w, seq_matmul(lambda j, r: _dot(q[r], wpast[j][0:KV_W].astype(MXU))), NEG)
    extra = new_keys(ATT_W + 4 * KV_W, lambda tk: tk <= step)
    o_w = softmax_two(s_win, extra, lambda j: wpast[j][KV_W:2 * KV_W].astype(MXU))

    o = o_c * gate[0] + o_s * gate[1] + o_w * gate[2]
    for j in range(ns):
        base = j * per_seq
        o_ref[j] = jnp.where(head_lanes[0], o[base:base + grp], o[base + grp:base + 2 * grp])
        rolled = pltpu.roll(wpast[j], win_past - n_new, 1)
        newr = pltpu.roll(wnewt_ref[j], LANES - n_new, 1)
        wout_ref[j, :, 0:win_past - LANES] = rolled[:, 0:win_past - LANES]
        wout_ref[j, :, win_past - LANES:win_past] = jnp.where(jl >= LANES - n_new, newr,
                                                              rolled[:, win_past - LANES:win_past])


def _attention_sample(page_table, cache_t, zrow, wnewt, wpast_t, kc, vc, cov, expand, *, layer, ns, pos0, n_new):
    bd, n_pages = page_table.shape
    n = n_pages * (PAGE // CMP_STRIDE)
    past = n_pages * PAGE
    n_sel = -(-(past + n_new) // SEL_BLOCK)
    win_past = wpast_t.shape[3]
    rows = Q_PER_KV * SUBLANES

    in_specs = ([pl.BlockSpec((1, 1, 2 * KV_W, PAGE), lambda i, pt, j=j, k=k: (layer, pt[i * ns + j, k], 1, 0))
                 for j in range(ns) for k in range(n_pages)]
                + [pl.BlockSpec((ns, SUBLANES, Z_ALL), lambda i, pt: (i, 0, 0)),
                   pl.BlockSpec((ns, 2 * KV_W, LANES), lambda i, pt: (i, 0, 0)),
                   pl.BlockSpec((1, ns, 2 * KV_W, win_past), lambda i, pt: (layer, i, 0, 0)),
                   pl.BlockSpec((ns, n, LANES), lambda i, pt: (i, 0, 0)),
                   pl.BlockSpec((ns, n, LANES), lambda i, pt: (i, 0, 0)),
                   pl.BlockSpec((n, LANES), lambda i, pt: (0, 0)),
                   pl.BlockSpec((LANES, past), lambda i, pt: (0, 0))])
    est = (2 * ns * n_pages * 2 * KV_W * PAGE * 4 + ns * 8 * past * KV_W * 4 + 8 * ns * 2 * KV_W * win_past * 4
           + (6 << 20))
    return pl.pallas_call(
        functools.partial(_attn_sample_kernel, n_pages=n_pages, ns=ns, n_new=n_new, pos0=pos0, n_sel=n_sel,
                          win_past=win_past),
        grid_spec=pltpu.PrefetchScalarGridSpec(
            num_scalar_prefetch=1,
            grid=(bd // ns,),
            in_specs=in_specs,
            out_specs=(pl.BlockSpec((ns, rows, KV_W), lambda i, pt: (i, 0, 0)),
                       pl.BlockSpec((ns, 2 * KV_W, win_past), lambda i, pt: (i, 0, 0)))),
        out_shape=(jax.ShapeDtypeStruct((bd, rows, KV_W), F32),
                   jax.ShapeDtypeStruct((bd, 2 * KV_W, win_past), F32)),
        compiler_params=_cparams(est),
        name="attention_sample",
    )(page_table, *([cache_t] * (ns * n_pages)), zrow, wnewt, wpast_t, kc, vc, cov, expand)


def _merge_kernel(x_ref, yp_ref, yn_ref, yc_ref, sg_ref, wbp_ref, wbn_ref, wbc_ref, wo_ref, o_ref):
    d = D_MODEL
    merged = (sg_ref[:, 0:d].astype(F32) * _dot(yp_ref[...], wbp_ref[...])
              + sg_ref[:, d:2 * d].astype(F32) * _dot(yn_ref[...], wbn_ref[...])
              + sg_ref[:, 2 * d:3 * d].astype(F32) * _dot(yc_ref[...], wbc_ref[...]))
    o_ref[...] = x_ref[...] + _dot(merged.astype(MXU), wo_ref[...])


def _merge(x, yp, yn, yc, sg, wbp, wbn, wbc, wo, *, tm):
    m = x.shape[0]
    row = lambda w: pl.BlockSpec((tm, w), lambda i: (i, 0))
    est = 2 * tm * (2 * D_MODEL * 4 + 1024 * 2 + 3072 * 2) + 2 * 3 * D_MODEL * D_MODEL + 6 * tm * D_MODEL * 4 + (2 << 20)
    return pl.pallas_call(
        _merge_kernel,
        grid=(m // tm,),
        in_specs=[row(D_MODEL), row(POOL_W), row(ATT_W), row(CONV_DIM), row(3 * D_MODEL),
                  _resident((POOL_W, D_MODEL)), _resident((ATT_W, D_MODEL)), _resident((CONV_DIM, D_MODEL)),
                  _resident((D_MODEL, D_MODEL))],
        out_specs=row(D_MODEL),
        out_shape=jax.ShapeDtypeStruct((m, D_MODEL), F32),
        compiler_params=_cparams(est),
        name="merge",
    )(x, yp, yn, yc, sg, wbp, wbn, wbc, wo)


def _ffn_kernel(x_ref, g_ref, pre_ref, wup_ref, wcv_ref, wdn_ref, gf_ref, o_ref, last_ref, *rest,
                tm, stride, tiles_per_seq, final):
    yf_ref = rest[0] if final else None
    aext = rest[-1]
    i = pl.program_id(0)
    hr = aext.shape[0] - tm

    @pl.when(i % tiles_per_seq == 0)
    def _():
        aext[0:hr, :] = pre_ref[...]

    x = x_ref[...]
    h = x * lax.rsqrt(jnp.mean(x * x, axis=-1, keepdims=True) + EPS)
    h = (h * g_ref[...]).astype(MXU)
    acc = x
    for j in range(D_FF // FF_CHUNK):
        cj = slice(j * FF_CHUNK, (j + 1) * FF_CHUNK)
        a = _dot(h, wup_ref[:, cj])
        aext[hr:hr + tm, cj] = a
        a1 = aext[hr - stride:hr - stride + tm, cj]
        a2 = aext[hr - 2 * stride:hr - 2 * stride + tm, cj]
        ac = wcv_ref[0:1, cj] * a2 + wcv_ref[1:2, cj] * a1 + wcv_ref[2:3, cj] * a
        bgate = _dot(h, wup_ref[:, D_FF + j * FF_CHUNK:D_FF + (j + 1) * FF_CHUNK])
        gated = (ac * _sigmoid(ac) * bgate).astype(MXU)
        acc = acc + _dot(gated, wdn_ref[cj, :])
    o_ref[...] = acc
    if final:
        yf = acc * lax.rsqrt(jnp.mean(acc * acc, axis=-1, keepdims=True) + EPS)
        yf_ref[...] = yf * gf_ref[...]
    keep = (CONV_TAPS - 1) * stride
    last_ref[0] = aext[hr + tm - keep:hr + tm, :]
    if tiles_per_seq > 1:
        aext[0:hr, :] = aext[tm:tm + hr, :]


def _conv_ffn(x, g, pre, wup, wcv, wdn, gf, *, tm, stride, tiles_per_seq, final):
    m = x.shape[0]
    nt = m // tm
    nseq = nt // tiles_per_seq
    hr = pre.shape[0] // nseq
    keep = (CONV_TAPS - 1) * stride
    row = pl.BlockSpec((tm, D_MODEL), lambda i: (i, 0))
    est = (3 * 2 * tm * D_MODEL * 4 + wup.size * 2 + wdn.size * 2 + (hr + tm) * D_FF * 4 + 2 * hr * D_FF * 4
           + 2 * keep * D_FF * 4 + 8 * tm * D_MODEL * 4 + (2 << 20))
    out_specs = [row, pl.BlockSpec((1, keep, D_FF), lambda i: (i // tiles_per_seq, 0, 0))]
    out_shape = [jax.ShapeDtypeStruct((m, D_MODEL), F32), jax.ShapeDtypeStruct((nseq, keep, D_FF), F32)]
    if final:
        out_specs.append(row)
        out_shape.append(jax.ShapeDtypeStruct((m, D_MODEL), F32))
    res = pl.pallas_call(
        functools.partial(_ffn_kernel, tm=tm, stride=stride, tiles_per_seq=tiles_per_seq, final=final),
        grid=(nt,),
        in_specs=[row, pl.BlockSpec((1, D_MODEL), lambda i: (0, 0)),
                  pl.BlockSpec((hr, D_FF), lambda i: (i // tiles_per_seq, 0)),
                  _resident((D_MODEL, 2 * D_FF)), pl.BlockSpec((CONV_TAPS, D_FF), lambda i: (0, 0)),
                  _resident((D_FF, D_MODEL)), pl.BlockSpec((1, D_MODEL), lambda i: (0, 0))],
        out_specs=tuple(out_specs),
        out_shape=tuple(out_shape),
        scratch_shapes=[pltpu.VMEM((hr + tm, D_FF), F32)],
        compiler_params=_cparams(est),
        name="conv_ffn",
    )(x, g, pre, wup, wcv, wdn, gf)
    return (res[0], res[1], res[2] if final else None)


def _rot_cols(w):
    k = w.shape[0]
    w4 = w.reshape(k, -1, 2, HEAD_DIM // 2)
    return jnp.stack([-w4[:, :, 1], w4[:, :, 0]], axis=2).reshape(k, -1)


def _rope_table_t(pos):
    half = HEAD_DIM // 2
    inv = ROPE_THETA ** (-jnp.arange(half, dtype=F32) / half)
    ang = pos.astype(F32)[:, None] * inv[None, :]
    return jnp.concatenate([jnp.tile(jnp.cos(ang), (1, 4)), jnp.tile(jnp.sin(ang), (1, 4))], axis=1).T


def _blockdiag2(w):
    z = jnp.zeros_like(w)
    return jnp.concatenate([jnp.concatenate([w, z], axis=1), jnp.concatenate([z, w], axis=1)], axis=0)


def _head_major_to_pair_major(w, axis):
    shp = w.shape
    w = w.reshape(shp[:axis] + (N_KV, Q_PER_KV, HEAD_DIM) + shp[axis + 1:])
    w = jnp.swapaxes(w, axis, axis + 1)
    return w.reshape(shp)


def _layer_weights(w_in, pool_w, cmp_pe, cmp_w1, cmp_w2, w_br_nsa):
    parts = []
    off = 0
    for w in IN_SPLITS:
        parts.append(w_in[:, off:off + w])
        off += w
    u, q, kv, gate, cb, cc, cx, gp, gn, gc = parts
    kv6 = kv.reshape(D_MODEL, 6, KV_W)
    kcmp, vcmp, ksel, vsel, kwin, vwin = [kv6[:, s] for s in range(6)]
    wrow = jnp.concatenate([u, cb, cc, cx, gp, gn, gc], axis=1).astype(MXU)
    qs = _head_major_to_pair_major(q, 1) * (HEAD_DIM ** -0.5 * LOG2E)
    gpad = jnp.pad(gate, ((0, 0), (0, LANES - gate.shape[1])))
    wt = jnp.concatenate([qs, _rot_cols(qs), kcmp, vcmp, ksel, _rot_cols(ksel), vsel,
                          kwin, _rot_cols(kwin), vwin, gpad], axis=1).T.astype(MXU)
    wbn = _head_major_to_pair_major(w_br_nsa, 0).astype(MXU)

    pw = jnp.zeros((POOL_W, POOL_W), F32)
    gsz = POOL_W // len(POOL_WINDOWS)
    for gi in range(len(POOL_WINDOWS)):
        pw = pw.at[gi * gsz:(gi + 1) * gsz, gi * gsz:(gi + 1) * gsz].set(pool_w[gi])

    wab = []
    peab = []
    for s in range(2):
        w1r = cmp_w1[s].reshape(CMP_LEN, HEAD_DIM, CMP_HIDDEN)
        halves = []
        for hsel in range(2):
            wh = w1r[hsel * CMP_STRIDE:(hsel + 1) * CMP_STRIDE]
            z = jnp.zeros((CMP_STRIDE, N_KV, HEAD_DIM, N_KV, CMP_HIDDEN), F32)
            for g in range(N_KV):
                z = z.at[:, g, :, g, :].set(wh)
            halves.append(z.reshape(CMP_STRIDE * KV_W, N_KV * CMP_HIDDEN))
            pe = cmp_pe[s][hsel * CMP_STRIDE:(hsel + 1) * CMP_STRIDE]
            peab.append(jnp.broadcast_to(pe[:, None, :], (CMP_STRIDE, N_KV, HEAD_DIM)).reshape(1, -1))
        wab.append(jnp.stack(halves))
    wab = jnp.stack(wab).astype(MXU)
    peab = jnp.concatenate(peab + [jnp.zeros((SUBLANES - 4, CMP_STRIDE * KV_W), F32)], axis=0)
    w2k = jnp.concatenate([_blockdiag2(cmp_w2[0]), _blockdiag2(_rot_cols(cmp_w2[0]))], axis=1).astype(MXU)
    w2v = _blockdiag2(cmp_w2[1]).astype(MXU)
    return dict(wrow=wrow, wt=wt, wbn=wbn, pw=pw.astype(MXU), wab=wab, peab=peab, w2k=w2k, w2v=w2v, w2vt=w2v.T)


def _cover(n_cmp, n_sel):
    c_start = jnp.arange(n_cmp) * CMP_STRIDE
    j_start = jnp.arange(n_sel) * SEL_BLOCK
    return ((c_start[:, None] < j_start[None, :] + SEL_BLOCK)
            & (c_start[:, None] + CMP_LEN > j_start[None, :])).astype(MXU)


def _feature_major(x):
    nd = x.ndim
    perm = tuple(range(nd - 4)) + (nd - 3, nd - 2, nd - 1, nd - 4)
    xt = jnp.transpose(x, perm)
    return xt.reshape(xt.shape[:nd - 4] + (-1, xt.shape[-1]))


def _token_major(xt, a, b, c):
    nd = xt.ndim
    x = xt.reshape(xt.shape[:nd - 2] + (a, b, c, xt.shape[-1]))
    perm = tuple(range(nd - 2)) + (nd + 1, nd - 2, nd - 1, nd)
    return jnp.transpose(x, perm)


def kernel(x_prompt, x_sample, cache_kv, cache_win, state_pool, state_conv, state_ffn, page_table,
           norm_mix, w_in, pool_w, pool_scale, cmp_pe, cmp_w1, cmp_w2, conv_w,
           w_br_pool, w_br_nsa, w_br_conv, w_out, norm_ffn, ffn_up, ffn_conv, ffn_down, norm_final):
    bp, sp, d = x_prompt.shape
    bd, sd, _ = x_sample.shape
    depth = w_in.shape[0]
    n_pages = page_table.shape[1]
    past = n_pages * PAGE
    wpast = cache_win.shape[2]
    assert d == D_MODEL and sp % (4 * Q_BLOCK) == 0 and sp >= WINDOW and sp // SEL_BLOCK <= LANES
    assert bd == LANES and sd & (sd - 1) == 0 and sd <= SUBLANES
    assert past % SEL_BLOCK == 0 and cache_kv.shape[2] == PAGE and wpast == WINDOW

    tm_p = 512
    mp = bp * sp
    md = bd * sd
    tps = sp // tm_p

    cst_p = _rope_table_t(jnp.arange(sp))
    cst_d = _rope_table_t(jnp.repeat(past + jnp.arange(sd), bd))
    nc_p = sp // CMP_STRIDE
    nc_d = past // CMP_STRIDE
    csc_p = _rope_table_t(jnp.arange(nc_p) * CMP_STRIDE + CMP_LEN - 1).T
    csc_d = _rope_table_t(jnp.arange(nc_d) * CMP_STRIDE + CMP_LEN - 1).T
    nsel_p = sp // SEL_BLOCK
    cov_pt = _cover(nc_p, nsel_p).T
    cov_d = _cover(nc_d, LANES)
    expand = (jnp.arange(LANES)[:, None] == (jnp.arange(past) // SEL_BLOCK)[None, :]).astype(MXU)
    onehot_p = ((jnp.arange(sp) // SEL_BLOCK)[:, None] == jnp.arange(LANES)[None, :]).astype(MXU)
    tok = jnp.arange(PAGE)
    perm = (tok[None, :] == (tok[:, None] % (PAGE // CMP_STRIDE)) * CMP_STRIDE + tok[:, None] // (PAGE // CMP_STRIDE))
    perm = perm.astype(MXU)

    hr_pool_p = 2 * SUBLANES
    hr_conv_p = SUBLANES
    zeros_pool = jnp.zeros((bp * hr_pool_p, POOL_W), F32)
    zeros_conv = jnp.zeros((bp * hr_conv_p, CONV_DIM), F32)
    zeros_ffn = jnp.zeros((bp * hr_conv_p, D_FF), F32)

    cache_t = _feature_major(cache_kv)
    win_t = _feature_major(cache_win)
    pool_tm = jnp.transpose(state_pool, (0, 2, 1, 3))

    xp = x_prompt.reshape(mp, d)
    xs = jnp.transpose(x_sample, (1, 0, 2)).reshape(md, d)

    keys = ("kv_p", "kv_s", "win_p", "win_s", "pool_p", "pool_s", "conv_p", "conv_s", "ffn_p", "ffn_s")
    outs = {k: [] for k in keys}
    yp_final = ys_final = None
    for l in range(depth):
        lw = _layer_weights(w_in[l], pool_w[l], cmp_pe[l], cmp_w1[l], cmp_w2[l], w_br_nsa[l])
        g_mix = norm_mix[l].reshape(1, d)
        g_ffn = norm_ffn[l].reshape(1, d)
        ps = pool_scale[l].reshape(1, POOL_W)
        cw = conv_w[l]
        wbp = w_br_pool[l].astype(MXU)
        wbc = w_br_conv[l].astype(MXU)
        wo = w_out[l].astype(MXU)
        wup = ffn_up[l].astype(MXU)
        wdn = ffn_down[l].astype(MXU)
        wcv = ffn_conv[l]
        gf = norm_final.reshape(1, d)
        final = l == depth - 1

        up, cb, ccx, sg, kvt, wint, qt, va, krows, gt = _in_proj(xp, g_mix, lw["wrow"], lw["wt"], cst_p,
                                                               tm=tm_p, seq=sp, sample=False)
        ypool, yconv = _mixers(up, cb, ccx, zeros_pool, zeros_conv, lw["pw"], ps, cw,
                               tm=tm_p, stride=1, tiles_per_seq=tps, pos0=0)
        kc, vct = _compress_prompt(kvt, perm, lw["peab"], lw["wab"], lw["w2k"], lw["w2vt"], csc_p)
        ynsa = _attention_prompt(qt, kc, vct, krows, va, gt, cov_pt, onehot_p, b=bp, s=sp)
        x1 = _merge(xp, ypool, ynsa, yconv, sg, wbp, lw["wbn"], wbc, wo, tm=tm_p)
        xp, a_last, yp_final = _conv_ffn(x1, g_ffn, zeros_ffn, wup, wcv, wdn, gf, tm=tm_p, stride=1,
                                         tiles_per_seq=tps, final=final)
        outs["kv_p"].append(kvt)
        outs["win_p"].append(wint[:, :, sp - WINDOW:])
        outs["pool_p"].append(up.reshape(bp, sp, POOL_W)[:, sp - (POOL_MAX - 1):])
        outs["conv_p"].append(ccx.reshape(bp, sp, CONV_DIM)[:, sp - (CONV_TAPS - 1):])
        outs["ffn_p"].append(a_last)

        up, cb, ccx, sg, zall = _in_proj(xs, g_mix, lw["wrow"], lw["wt"], cst_d, tm=md, seq=sd, sample=True)
        pool_pre = pool_tm[l].reshape((POOL_MAX - 1) * bd, POOL_W)
        conv_pre = jnp.transpose(state_conv[l], (1, 0, 2)).reshape((CONV_TAPS - 1) * bd, CONV_DIM)
        ffn_pre = jnp.transpose(state_ffn[l], (1, 0, 2)).reshape((CONV_TAPS - 1) * bd, D_FF)
        ypool, yconv = _mixers(up, cb, ccx, pool_pre, conv_pre, lw["pw"], ps, cw,
                               tm=md, stride=bd, tiles_per_seq=1, pos0=past)
        zrow = jnp.pad(jnp.transpose(zall, (2, 0, 1)), ((0, 0), (0, SUBLANES - sd), (0, 0)))
        wnew_t = zall[:, ATT_W + 4 * KV_W:ATT_W + 6 * KV_W, :]
        wnew_tp = jnp.pad(jnp.transpose(wnew_t, (2, 1, 0)), ((0, 0), (0, 0), (0, LANES - sd)))
        kc_d, vc_d = _compress_sample(page_table, cache_t, perm, lw["peab"], lw["wab"], lw["w2k"], lw["w2v"], csc_d,
                                      layer=l, nb=4)
        o_seq, win_new = _attention_sample(page_table, cache_t, zrow, wnew_tp, win_t, kc_d, vc_d, cov_d, expand,
                                           layer=l, ns=4, pos0=past, n_new=sd)
        ynsa = jnp.transpose(o_seq.reshape(bd, Q_PER_KV, SUBLANES, KV_W)[:, :, 0:sd], (2, 0, 1, 3))
        ynsa = ynsa.reshape(md, ATT_W).astype(MXU)
        x1 = _merge(xs, ypool, ynsa, yconv, sg, wbp, lw["wbn"], wbc, wo, tm=md)
        xs, a_last, ys_final = _conv_ffn(x1, g_ffn, ffn_pre, wup, wcv, wdn, gf, tm=md, stride=bd,
                                         tiles_per_seq=1, final=final)
        outs["kv_s"].append(zall[:, ATT_W:ATT_W + 4 * KV_W, :])
        outs["win_s"].append(win_new)
        up_tm = up.reshape(sd, bd, POOL_W)
        pool_full = jnp.concatenate([pool_tm[l], up_tm], axis=0)
        outs["pool_s"].append(pool_full[pool_full.shape[0] - (POOL_MAX - 1):])
        ccx_seq = jnp.transpose(ccx.reshape(sd, bd, CONV_DIM), (1, 0, 2))
        conv_full = jnp.concatenate([state_conv[l], ccx_seq], axis=1)
        outs["conv_s"].append(conv_full[:, conv_full.shape[1] - (CONV_TAPS - 1):])
        outs["ffn_s"].append(jnp.transpose(a_last.reshape(CONV_TAPS - 1, bd, D_FF), (1, 0, 2)))

    st = lambda k: jnp.stack(outs[k])
    y_prompt = yp_final.reshape(bp, sp, d)
    y_sample = jnp.transpose(ys_final.reshape(sd, bd, d), (1, 0, 2))
    kv_prompt = _token_major(st("kv_p"), 4, N_KV, HEAD_DIM)
    kv_s = st("kv_s").reshape(depth, sd, 4, N_KV, HEAD_DIM, bd)
    kv_sample = jnp.transpose(kv_s, (0, 5, 1, 2, 3, 4))
    win_prompt = _token_major(st("win_p"), 2, N_KV, HEAD_DIM)
    win_sample = _token_major(st("win_s"), 2, N_KV, HEAD_DIM)
    pool_sample = jnp.transpose(st("pool_s"), (0, 2, 1, 3))
    return (y_prompt, y_sample, kv_prompt, kv_sample, win_prompt, win_sample, st("pool_p"), pool_sample,
            st("conv_p"), st("conv_s"), st("ffn_p"), st("ffn_s"))
```

```python
import functools
import math

import jax
import jax.numpy as jnp
from jax import lax
from jax.experimental import pallas as pl
from jax.experimental.pallas import tpu as pltpu

D_MODEL = 1024
HEAD_DIM = 64
N_HEADS = 8
N_KV = 2
Q_PER_KV = 4
ATT_W = N_HEADS * HEAD_DIM
KV_W = N_KV * HEAD_DIM
CMP_LEN = 32
CMP_STRIDE = 16
CMP_HIDDEN = 128
SEL_BLOCK = 64
SEL_TOPK = 16
N_LOCAL = 2
WINDOW = 512
Q_BLOCK = 128
FORCED_SCORE = 1e4
POOL_WINDOWS = (2, 4, 8, 16)
POOL_W = 256
POOL_MAX = 16
CONV_TAPS = 3
CONV_DIM = 256
D_FF = 2816
ROPE_THETA = 10000.0
EPS = 1e-6
PAGE = 128
IN_SPLITS = (POOL_W, ATT_W, 6 * KV_W, 3 * N_HEADS, CONV_DIM, CONV_DIM, CONV_DIM, D_MODEL, D_MODEL, D_MODEL)

LANES = 128
SUBLANES = 8
BF16_ROWS = 16
V7X_VMEM_BYTES = 64 * 1024 * 1024
VMEM_CAP = 56 * 1024 * 1024

NEG = -1e30
MXU = jnp.bfloat16
F32 = jnp.float32
LOG2E = math.log2(math.e)

W_ROW = POOL_W + 3 * CONV_DIM + 3 * D_MODEL
T_Q = 2 * ATT_W
T_KV = 5 * KV_W
T_WIN = 3 * KV_W
W_T = T_Q + T_KV + T_WIN + LANES
VA_ROWS = HEAD_DIM + BF16_ROWS
Z_ALL = ATT_W + 4 * KV_W + 2 * KV_W + LANES
FF_CHUNK = 256


def _cparams(est_bytes, ndim=1):
    limit = int(min(max(est_bytes, 16 * 1024 * 1024), VMEM_CAP))
    return pltpu.CompilerParams(dimension_semantics=("arbitrary",) * ndim, vmem_limit_bytes=limit)


def _resident(shape):
    n = len(shape)
    return pl.BlockSpec(shape, lambda *_: (0,) * n, pipeline_mode=pl.Buffered(1))


def _sigmoid(x):
    return 1.0 / (1.0 + jnp.exp(-x))


def _dot(a, b):
    return jnp.dot(a, b, preferred_element_type=F32)


def _dot_nt(a, b):
    return lax.dot_general(a, b, (((1,), (1,)), ((), ())), preferred_element_type=F32)


def _split3(x):
    hi = x.astype(MXU)
    r1 = x - hi.astype(F32)
    mid = r1.astype(MXU)
    lo = (r1 - mid.astype(F32)).astype(MXU)
    return hi, mid, lo


def _dot3(x, w):
    out = None
    for term in _split3(x):
        t = _dot(term, w)
        out = t if out is None else out + t
    return out


def _inproj_kernel(x_ref, g_ref, wrow_ref, wt_ref, cst_ref, *rest, tm, sample, has_prev):
    up_ref, cb_ref, ccx_ref, sg_ref, *outs = rest[1:] if has_prev else rest
    x = x_ref[...]
    h = x * lax.rsqrt(jnp.mean(x * x, axis=-1, keepdims=True) + EPS)
    h = (h * g_ref[...]).astype(MXU)

    up_ref[...] = _dot(h, wrow_ref[:, 0:POOL_W])
    z = _dot(h, wrow_ref[:, POOL_W:POOL_W + 3 * CONV_DIM])
    cb_ref[...] = z[:, 0:256]
    ccx_ref[...] = z[:, 256:512] * z[:, 512:768]
    g0 = POOL_W + 3 * CONV_DIM
    for j in range(6):
        z = _dot(h, wrow_ref[:, g0 + 512 * j:g0 + 512 * (j + 1)])
        sg_ref[:, 512 * j:512 * (j + 1)] = _sigmoid(z).astype(sg_ref.dtype)

    cost = cst_ref[0:LANES, :]
    sint = cst_ref[LANES:2 * LANES, :]
    zq = _dot_nt(wt_ref[0:T_Q, :], h)
    q = jnp.concatenate(
        [zq[i * LANES:(i + 1) * LANES] * cost + zq[ATT_W + i * LANES:ATT_W + (i + 1) * LANES] * sint
         for i in range(ATT_W // LANES)], axis=0)
    zk = _dot_nt(wt_ref[T_Q:T_Q + T_KV, :], h)
    ksel = zk[256:384] * cost + zk[384:512] * sint
    kv = jnp.concatenate([zk[0:256], ksel, zk[512:640]], axis=0)
    zw = _dot_nt(wt_ref[T_Q + T_KV:T_Q + T_KV + T_WIN, :], h)
    kwin = zw[0:128] * cost + zw[128:256] * sint
    win = jnp.concatenate([kwin, zw[256:384]], axis=0)
    gates = _sigmoid(_dot_nt(wt_ref[T_Q + T_KV + T_WIN:W_T, :], h))

    if sample:
        (zall_ref,) = outs
        for c in range(tm // LANES):
            cols = slice(c * LANES, (c + 1) * LANES)
            zall_ref[c] = jnp.concatenate([q[:, cols], kv[:, cols], win[:, cols], gates[:, cols]], axis=0)
    else:
        kvt_ref, wint_ref, qt_ref, va_ref, krows_ref, gt_ref = outs
        kvt_ref[0, 0] = kv
        wint_ref[0] = win
        ones = jnp.ones((BF16_ROWS, LANES), va_ref.dtype)
        vsel = kv[384:512].astype(va_ref.dtype)
        vwin = win[128:256].astype(va_ref.dtype)
        for c in range(tm // LANES):
            cols = slice(c * LANES, (c + 1) * LANES)
            qt_ref[c] = q[:, cols].astype(qt_ref.dtype)
            va_ref[c] = jnp.concatenate(
                [vsel[0:64, cols], ones, vsel[64:128, cols], ones, vwin[0:64, cols], ones, vwin[64:128, cols], ones],
                axis=0)
            gt_ref[c] = gates[:, cols]
        krows_ref[:, 0:KV_W] = ksel.T.astype(krows_ref.dtype)
        krows_ref[:, KV_W:2 * KV_W] = kwin.T.astype(krows_ref.dtype)


def _in_proj(x, g, wrow, wt, cst, *, tm, seq, sample, layer=0, depth=1, kv_prev=None):
    m = x.shape[0]
    nt = m // tm
    ntab = cst.shape[1] // tm
    nb = tm // LANES
    tps = seq // tm if not sample else 1
    row = lambda w: pl.BlockSpec((tm, w), lambda i: (i, 0))
    blk = lambda r: pl.BlockSpec((nb, r, LANES), lambda i: (i, 0, 0))
    out_shape = [jax.ShapeDtypeStruct((m, POOL_W), F32), jax.ShapeDtypeStruct((m, CONV_DIM), F32),
                 jax.ShapeDtypeStruct((m, CONV_DIM), F32), jax.ShapeDtypeStruct((m, 3 * D_MODEL), MXU)]
    out_specs = [row(POOL_W), row(CONV_DIM), row(CONV_DIM), row(3 * D_MODEL)]
    if sample:
        out_shape += [jax.ShapeDtypeStruct((m // LANES, Z_ALL, LANES), F32)]
        out_specs += [blk(Z_ALL)]
    else:
        out_shape += [jax.ShapeDtypeStruct((depth, m // seq, 4 * KV_W, seq), F32),
                      jax.ShapeDtypeStruct((m // seq, 2 * KV_W, seq), F32),
                      jax.ShapeDtypeStruct((m // LANES, ATT_W, LANES), MXU),
                      jax.ShapeDtypeStruct((m // LANES, 4 * VA_ROWS, LANES), MXU),
                      jax.ShapeDtypeStruct((m, 2 * KV_W), MXU),
                      jax.ShapeDtypeStruct((m // LANES, LANES, LANES), F32)]
        out_specs += [pl.BlockSpec((1, 1, 4 * KV_W, tm), lambda i: (layer, i // tps, 0, i % tps)),
                      pl.BlockSpec((1, 2 * KV_W, tm), lambda i: (i // tps, 0, i % tps)),
                      blk(ATT_W), blk(4 * VA_ROWS), row(2 * KV_W), blk(LANES)]
    est = (2 * tm * D_MODEL * 4 + wrow.size * 2 + wt.size * 2 + 4 * tm * 256 * 4
           + 2 * tm * (3 * 256 * 4 + 3072 * 2) + 2 * tm * Z_ALL * 4 + tm * (T_Q + T_KV + T_WIN + 1024 + 1024) * 4
           + (4 << 20))
    in_specs = [pl.BlockSpec((tm, D_MODEL), lambda i: (i, 0)),
                pl.BlockSpec((1, D_MODEL), lambda i: (0, 0)),
                _resident((D_MODEL, W_ROW)),
                _resident((W_T, D_MODEL)),
                pl.BlockSpec((2 * LANES, tm), lambda i: (0, i % ntab))]
    args = [x, g, wrow, wt, cst]
    aliases = {}
    if kv_prev is not None:
        aliases = {len(args): 4}
        in_specs.append(pl.BlockSpec(memory_space=pl.ANY))
        args.append(kv_prev)
    return pl.pallas_call(
        functools.partial(_inproj_kernel, tm=tm, sample=sample, has_prev=kv_prev is not None),
        grid=(nt,),
        in_specs=in_specs,
        out_specs=tuple(out_specs),
        out_shape=tuple(out_shape),
        input_output_aliases=aliases,
        compiler_params=_cparams(est),
        name="in_proj",
    )(*args)


def _mix_kernel(up_ref, cb_ref, ccx_ref, ppre_ref, cpre_ref, pw_ref, ps_ref, cw_ref,
                yp_ref, yc_ref, pext, cext, *, tm, stride, tiles_per_seq, pos0):
    i = pl.program_id(0)
    hp = pext.shape[0] - tm
    hc = cext.shape[0] - tm

    @pl.when(i % tiles_per_seq == 0)
    def _():
        pext[0:hp, :] = ppre_ref[...]
        cext[0:hc, :] = cpre_ref[...]

    u = up_ref[...]
    pext[hp:hp + tm, :] = u
    acc = u
    sums = {}
    for k in range(1, POOL_MAX):
        acc = acc + pext[hp - k * stride:hp - k * stride + tm, :]
        if k + 1 in POOL_WINDOWS:
            sums[k + 1] = acc
    row = lax.broadcasted_iota(jnp.int32, (tm, 1), 0)
    if stride == 1:
        t_abs = pos0 + (i % tiles_per_seq) * tm + row
    else:
        t_abs = pos0 + lax.shift_right_logical(row, stride.bit_length() - 1)
    tp1 = (t_abs + 1).astype(F32)
    lane = lax.broadcasted_iota(jnp.int32, (1, POOL_W), 1)
    grp = POOL_W // len(POOL_WINDOWS)
    mean = None
    for gi, w in reversed(list(enumerate(POOL_WINDOWS))):
        mw = sums[w] / jnp.minimum(float(w), tp1)
        mean = mw if mean is None else jnp.where(lane < (gi + 1) * grp, mw, mean)
    pooled = mean - u
    y = _dot(pooled.astype(MXU), pw_ref[...]) * ps_ref[...]
    yp_ref[...] = y.astype(yp_ref.dtype)

    e0 = ccx_ref[...]
    cext[hc:hc + tm, :] = e0
    e1 = cext[hc - stride:hc - stride + tm, :]
    e2 = cext[hc - 2 * stride:hc - 2 * stride + tm, :]
    conv = cw_ref[0:1, :] * e2 + cw_ref[1:2, :] * e1 + cw_ref[2:3, :] * e0
    yc_ref[...] = (cb_ref[...] * conv).astype(yc_ref.dtype)

    if tiles_per_seq > 1:
        pext[0:hp, :] = pext[tm:tm + hp, :]
        cext[0:hc, :] = cext[tm:tm + hc, :]


def _mixers(up, cb, ccx, ppre, cpre, pw, ps, cw, *, tm, stride, tiles_per_seq, pos0):
    m = up.shape[0]
    nt = m // tm
    hp = ppre.shape[0] // (nt // tiles_per_seq)
    hc = cpre.shape[0] // (nt // tiles_per_seq)
    row = pl.BlockSpec((tm, 256), lambda i: (i, 0))
    est = 2 * 5 * tm * 256 * 4 + (2 * tm + hp + hc) * 256 * 4 + 2 * (hp + hc) * 256 * 4 + 12 * tm * 256 * 4 + (2 << 20)
    return pl.pallas_call(
        functools.partial(_mix_kernel, tm=tm, stride=stride, tiles_per_seq=tiles_per_seq, pos0=pos0),
        grid=(nt,),
        in_specs=[row, row, row,
                  pl.BlockSpec((hp, 256), lambda i: (i // tiles_per_seq, 0)),
                  pl.BlockSpec((hc, 256), lambda i: (i // tiles_per_seq, 0)),
                  pl.BlockSpec((256, 256), lambda i: (0, 0)),
                  pl.BlockSpec((1, 256), lambda i: (0, 0)),
                  pl.BlockSpec((CONV_TAPS, 256), lambda i: (0, 0))],
        out_specs=(row, row),
        out_shape=(jax.ShapeDtypeStruct((m, 256), MXU), jax.ShapeDtypeStruct((m, 256), MXU)),
        scratch_shapes=[pltpu.VMEM((hp + tm, 256), F32), pltpu.VMEM((hc + tm, 256), F32)],
        compiler_params=_cparams(est),
        name="mixers",
    )(up, cb, ccx, ppre, cpre, pw, ps, cw)


def _gelu_tanh(x):
    return x * (0.5 * (1.0 + jnp.tanh(0.7978845608028654 * (x + 0.044715 * (x * x * x)))))


def _compress_bias(peab_ref, wab_ref):
    pe3 = _split3(peab_ref[...])
    out = []
    for s in range(2):
        ba = None
        bb = None
        for term in pe3:
            ta = _dot(term, wab_ref[s, 0])
            tb = _dot(term, wab_ref[s, 1])
            ba = ta if ba is None else ba + ta
            bb = tb if bb is None else bb + tb
        out.append(ba[2 * s:2 * s + 1, :] + bb[2 * s + 1:2 * s + 2, :])
    return out


def _compress_hidden(get_page, n_pages, perm_ref, wab_ref, bias, pb_scr):
    n = n_pages * (PAGE // CMP_STRIDE)
    per = PAGE // CMP_STRIDE
    ys = ([], [])
    for p in range(n_pages):
        yp = _dot_nt(perm_ref[...], get_page(p).astype(MXU))
        for s in range(2):
            ys[s].append(jnp.concatenate(
                [yp[t * per:(t + 1) * per, s * KV_W:(s + 1) * KV_W] for t in range(CMP_STRIDE)], axis=1))
    acts = []
    for s in range(2):
        rows = jnp.concatenate(ys[s], axis=0).astype(MXU)
        pa = _dot(rows, wab_ref[s, 0])
        pb = _dot(rows, wab_ref[s, 1])
        pb_scr[0:n, :] = pb
        pb_scr[n:n + SUBLANES, :] = jnp.zeros((SUBLANES, 2 * CMP_HIDDEN), F32)
        pre = pa + pb_scr[1:n + 1, :] + bias[s]
        acts.append(_gelu_tanh(pre).astype(MXU))
    return acts


def _cmp_prompt_kernel(kvt_ref, perm_ref, peab_ref, wab_ref, w2k_ref, w2vt_ref, csc_ref, kc_ref, vct_ref, pb_scr,
                       *, n_pages):
    def get_page(p):
        return kvt_ref[0, 0, :, p * PAGE:(p + 1) * PAGE]

    bias = _compress_bias(peab_ref, wab_ref)
    act_k, act_v = _compress_hidden(get_page, n_pages, perm_ref, wab_ref, bias, pb_scr)
    yk = _dot(act_k, w2k_ref[...])
    kc = yk[:, 0:LANES] * csc_ref[:, 0:LANES] + yk[:, LANES:2 * LANES] * csc_ref[:, LANES:2 * LANES]
    kc_ref[0] = kc.astype(kc_ref.dtype)
    vct_ref[0] = _dot_nt(w2vt_ref[...], act_v).astype(vct_ref.dtype)


def _cmp_sample_kernel(pt_ref, *refs, n_pages, nb):
    del pt_ref
    page_refs = refs[0:nb * n_pages]
    perm_ref, peab_ref, wab_ref, w2k_ref, w2v_ref, csc_ref, kc_ref, vc_ref, pb_scr, bias_scr = refs[nb * n_pages:]
    n = n_pages * (PAGE // CMP_STRIDE)

    @pl.when(pl.program_id(0) == 0)
    def _():
        for s, b in enumerate(_compress_bias(peab_ref, wab_ref)):
            bias_scr[s] = jnp.broadcast_to(b, (SUBLANES, 2 * CMP_HIDDEN))

    bias = [bias_scr[s, 0:1, :] for s in range(2)]
    act_k, act_v = _compress_hidden(lambda p: page_refs[p][0, 0], nb * n_pages, perm_ref, wab_ref, bias, pb_scr)
    yk = _dot(act_k, w2k_ref[...])
    vc = _dot(act_v, w2v_ref[...])
    for j in range(nb):
        r = slice(j * n, (j + 1) * n)
        kc = yk[r, 0:LANES] * csc_ref[:, 0:LANES] + yk[r, LANES:2 * LANES] * csc_ref[:, LANES:2 * LANES]
        kc_ref[j] = kc.astype(kc_ref.dtype)
        vc_ref[j] = vc[r].astype(vc_ref.dtype)


def _compress_sample(page_table, cache_t, perm, peab, wab, w2k, w2v, csc, *, layer, nb):
    bd, n_pages = page_table.shape
    n = n_pages * (PAGE // CMP_STRIDE)
    in_specs = ([pl.BlockSpec((1, 1, 2 * KV_W, PAGE), lambda i, pt, j=j, k=k: (layer, pt[i * nb + j, k], 0, 0))
                 for j in range(nb) for k in range(n_pages)]
                + [pl.BlockSpec((PAGE, PAGE), lambda i, pt: (0, 0)),
                   pl.BlockSpec((SUBLANES, 2048), lambda i, pt: (0, 0)),
                   pl.BlockSpec((2, 2, 2048, 256), lambda i, pt: (0, 0, 0, 0), pipeline_mode=pl.Buffered(1)),
                   pl.BlockSpec((256, 256), lambda i, pt: (0, 0)),
                   pl.BlockSpec((256, 128), lambda i, pt: (0, 0)),
                   pl.BlockSpec((n, 256), lambda i, pt: (0, 0))])
    out = pl.BlockSpec((nb, n, LANES), lambda i, pt: (i, 0, 0))
    est = 2 * nb * n_pages * 2 * KV_W * PAGE * 4 + wab.size * 2 + 10 * nb * n * 2048 * 4 + (4 << 20)
    return pl.pallas_call(
        functools.partial(_cmp_sample_kernel, n_pages=n_pages, nb=nb),
        grid_spec=pltpu.PrefetchScalarGridSpec(
            num_scalar_prefetch=1,
            grid=(bd // nb,),
            in_specs=in_specs,
            out_specs=(out, out),
            scratch_shapes=[pltpu.VMEM((nb * n + SUBLANES, 2 * CMP_HIDDEN), F32),
                            pltpu.VMEM((2, SUBLANES, 2 * CMP_HIDDEN), F32)]),
        out_shape=(jax.ShapeDtypeStruct((bd, n, LANES), MXU), jax.ShapeDtypeStruct((bd, n, LANES), MXU)),
        compiler_params=_cparams(est),
        name="compress_sample",
    )(page_table, *([cache_t] * (nb * n_pages)), perm, peab, wab, w2k, w2v, csc)


def _compress_prompt(kvt, perm, peab, wab, w2k, w2vt, csc, *, layer):
    _, b, _, s = kvt.shape
    n_pages = s // PAGE
    n = s // CMP_STRIDE
    est = 2 * 256 * s * 4 + wab.size * 2 + 8 * n * 2048 * 4 + (4 << 20)
    return pl.pallas_call(
        functools.partial(_cmp_prompt_kernel, n_pages=n_pages),
        grid=(b,),
        in_specs=[pl.BlockSpec((1, 1, 2 * KV_W, s), lambda i: (layer, i, 0, 0)),
                  pl.BlockSpec((PAGE, PAGE), lambda i: (0, 0)),
                  pl.BlockSpec((SUBLANES, 2048), lambda i: (0, 0)),
                  _resident((2, 2, 2048, 256)),
                  pl.BlockSpec((256, 256), lambda i: (0, 0)),
                  pl.BlockSpec((128, 256), lambda i: (0, 0)),
                  pl.BlockSpec((n, 256), lambda i: (0, 0))],
        out_specs=(pl.BlockSpec((1, n, LANES), lambda i: (i, 0, 0)),
                   pl.BlockSpec((1, LANES, n), lambda i: (i, 0, 0))),
        out_shape=(jax.ShapeDtypeStruct((b, n, LANES), MXU), jax.ShapeDtypeStruct((b, LANES, n), MXU)),
        scratch_shapes=[pltpu.VMEM((n + SUBLANES, 2 * CMP_HIDDEN), F32)],
        compiler_params=_cparams(est),
        name="compress_prompt",
    )(kvt, perm, peab, wab, w2k, w2vt, csc)


def _attn_prompt_kernel(qt_ref, kc_ref, vct_ref, krows_ref, va_ref, gt_ref, cov_ref, oh_ref,
                        y_ref, qaug, score_scr, *, n_cmp, n_sel):
    qb = pl.program_id(1)
    q0 = qb * Q_BLOCK
    nq = N_HEADS * Q_BLOCK
    half = Q_PER_KV * Q_BLOCK

    qt = qt_ref[0]
    rsel = lax.broadcasted_iota(jnp.int32, (2 * HEAD_DIM, 1), 0) < HEAD_DIM
    zero = jnp.zeros((), qt.dtype)
    for g in range(N_KV):
        for r in range(Q_PER_KV):
            blk = qt[r * LANES:(r + 1) * LANES, :]
            keep = rsel if g == 0 else jnp.logical_not(rsel)
            qaug[0:LANES, (g * Q_PER_KV + r) * Q_BLOCK:(g * Q_PER_KV + r + 1) * Q_BLOCK] = jnp.where(keep, blk, zero)
    qbd = qaug[0:LANES, :]

    lane = lax.broadcasted_iota(jnp.int32, (1, nq), 1)
    qpos = q0 + jnp.bitwise_and(lane, Q_BLOCK - 1)

    sc = _dot(kc_ref[0], qbd)
    posc = lax.broadcasted_iota(jnp.int32, (n_cmp, 1), 0) * CMP_STRIDE + (CMP_LEN - 1)
    valid = posc <= qpos
    s = jnp.where(valid, sc, NEG)
    mx = jnp.max(s, axis=0, keepdims=True)
    mx = jnp.where(mx > 0.5 * NEG, mx, 0.0)
    e = jnp.where(valid, jnp.exp2(s - mx), 0.0)
    p = e / jnp.maximum(jnp.sum(e, axis=0, keepdims=True), 1.0)
    pm = p.astype(MXU)
    vct = vct_ref[0]
    o_cmp = [_dot(vct[g * HEAD_DIM:(g + 1) * HEAD_DIM, :], pm[:, g * half:(g + 1) * half]) for g in range(N_KV)]

    w2 = N_KV * Q_BLOCK
    jrow = lax.broadcasted_iota(jnp.int32, (n_sel, 1), 0)
    qp2 = q0 + jnp.bitwise_and(lax.broadcasted_iota(jnp.int32, (1, w2), 1), Q_BLOCK - 1)
    cur = lax.shift_right_logical(qp2, 6)
    forced = (jrow == 0) | ((jrow <= cur) & (jrow > cur - N_LOCAL))
    causal = jrow * SEL_BLOCK <= qp2
    psum = []
    for g in range(N_KV):
        ps = p[:, g * half:g * half + Q_BLOCK]
        for r in range(1, Q_PER_KV):
            ps = ps + p[:, g * half + r * Q_BLOCK:g * half + (r + 1) * Q_BLOCK]
        psum.append(ps)
    imp = None
    for term in _split3(jnp.concatenate(psum, axis=1)):
        t = _dot(cov_ref[...], term)
        imp = t if imp is None else imp + t
    score = jnp.where(forced, FORCED_SCORE, jnp.where(causal, imp, -1.0))
    score_scr[...] = score
    n_live = jnp.minimum(n_sel, 2 * qb + 2)

    rank_unroll = 4

    def rank_body(i, cnt):
        for u in range(rank_unroll):
            jp = i * rank_unroll + u
            rowv = score_scr[pl.ds(jp, 1), :]
            ahead = (rowv > score) | ((rowv == score) & (jp < jrow))
            cnt = cnt + jnp.where(ahead, 1.0, 0.0)
        return cnt

    cnt = lax.fori_loop(0, (n_live + rank_unroll - 1) // rank_unroll, rank_body, jnp.zeros((n_sel, w2), F32))
    bias = jnp.where((cnt < float(SEL_TOPK)) & causal, 0.0, NEG).astype(MXU)
    qaug[LANES:2 * LANES, :] = jnp.zeros((LANES, nq), MXU)
    qaug[LANES:LANES + n_sel, :] = jnp.concatenate(
        [bias[:, 0:Q_BLOCK]] * Q_PER_KV + [bias[:, Q_BLOCK:w2]] * Q_PER_KV, axis=1)

    def step(state, k0, kk, rhs, vrow0, visible):
        m, a0, a1 = state
        nk = kk.shape[0]
        sij = _dot(kk, rhs)
        if visible is not None:
            kp = k0 + lax.broadcasted_iota(jnp.int32, (nk, 1), 0)
            sij = jnp.where(visible(kp), sij, NEG)
        m_new = jnp.maximum(m, jnp.max(sij, axis=0, keepdims=True))
        alpha = jnp.exp2(m - m_new)
        pij = jnp.exp2(sij - m_new).astype(MXU)
        vblk = jnp.concatenate([va_ref[k0 // LANES + sub][vrow0:vrow0 + 2 * VA_ROWS, :]
                                for sub in range(nk // LANES)], axis=1)
        a0 = alpha[:, 0:half] * a0 + _dot(vblk[0:VA_ROWS], pij[:, 0:half])
        a1 = alpha[:, half:nq] * a1 + _dot(vblk[VA_ROWS:2 * VA_ROWS], pij[:, half:nq])
        return m_new, a0, a1

    def combine(sa, sb):
        ma, a0a, a1a = sa
        mb, a0b, a1b = sb
        mm = jnp.maximum(ma, mb)
        wa = jnp.exp2(ma - mm)
        wb = jnp.exp2(mb - mm)
        return (mm, wa[:, 0:half] * a0a + wb[:, 0:half] * a0b, wa[:, half:nq] * a1a + wb[:, half:nq] * a1b)

    def merge(*states):
        st = states[0]
        for other in states[1:]:
            st = combine(st, other)
        return [a[0:HEAD_DIM] * (1.0 / a[HEAD_DIM:HEAD_DIM + 1]) for a in st[1:]]

    init = (jnp.full((1, nq), NEG, F32), jnp.zeros((VA_ROWS, half), F32), jnp.zeros((VA_ROWS, half), F32))

    sc_keys = 2 * Q_BLOCK

    def sel_step(state, c, causal_mask):
        k0 = pl.multiple_of(c * sc_keys, sc_keys)
        kk = jnp.concatenate([krows_ref[pl.ds(k0, sc_keys), 0:KV_W], oh_ref[pl.ds(k0, sc_keys), :]], axis=1)
        return step(state, k0, kk, qaug[...], 0, (lambda kp: kp <= qpos) if causal_mask else None)

    n_pairs = qb // 4
    sa, sb = lax.fori_loop(0, n_pairs,
                           lambda i, st: (sel_step(st[0], 2 * i, False), sel_step(st[1], 2 * i + 1, False)),
                           (init, init))
    sel_o = merge(sel_step(sa, 2 * n_pairs, True), sel_step(sb, 2 * n_pairs + 1, True))

    n_back = WINDOW // Q_BLOCK

    def win_step(state, c):
        blk = qb - n_back + c
        ok = blk >= 0
        k0 = pl.multiple_of(jnp.maximum(blk, 0) * Q_BLOCK, Q_BLOCK)
        kk = krows_ref[pl.ds(k0, Q_BLOCK), KV_W:2 * KV_W]
        return step(state, k0, kk, qbd, 2 * VA_ROWS, lambda kp: (kp <= qpos) & (kp >= qpos - WINDOW) & ok)

    n_win_states = 2
    wst = [init] * n_win_states
    for c in range(n_back + 1):
        wst[c % n_win_states] = win_step(wst[c % n_win_states], c)
    win_o = merge(*wst)

    gt = gt_ref[0]
    for r in range(Q_PER_KV):
        c = slice(r * Q_BLOCK, (r + 1) * Q_BLOCK)
        pair = []
        for g in range(N_KV):
            hh = g * Q_PER_KV + r
            pair.append(o_cmp[g][:, c] * gt[hh:hh + 1, :] + sel_o[g][:, c] * gt[N_HEADS + hh:N_HEADS + hh + 1, :]
                        + win_o[g][:, c] * gt[2 * N_HEADS + hh:2 * N_HEADS + hh + 1, :])
        y_ref[:, r * LANES:(r + 1) * LANES] = jnp.concatenate(pair, axis=0).T.astype(y_ref.dtype)


def _attention_prompt(qt, kc, vct, krows, va, gt, cov, onehot, *, b, s):
    nqb = s // Q_BLOCK
    n_cmp = kc.shape[1]
    n_sel = s // SEL_BLOCK
    est = (2 * s * 2 * KV_W * 2 + 2 * (s // LANES) * 4 * VA_ROWS * LANES * 2 + s * LANES * 2 + 10 * 256 * 1024 * 4
           + 5 * n_cmp * 1024 * 4 + (6 << 20))
    return pl.pallas_call(
        functools.partial(_attn_prompt_kernel, n_cmp=n_cmp, n_sel=n_sel),
        grid=(b, nqb),
        in_specs=[pl.BlockSpec((1, ATT_W, LANES), lambda i, j: (i * nqb + j, 0, 0)),
                  pl.BlockSpec((1, n_cmp, LANES), lambda i, j: (i, 0, 0)),
                  pl.BlockSpec((1, LANES, n_cmp), lambda i, j: (i, 0, 0)),
                  pl.BlockSpec((s, 2 * KV_W), lambda i, j: (i, 0)),
                  pl.BlockSpec((s // LANES, 4 * VA_ROWS, LANES), lambda i, j: (i, 0, 0)),
                  pl.BlockSpec((1, LANES, LANES), lambda i, j: (i * nqb + j, 0, 0)),
                  pl.BlockSpec((n_sel, n_cmp), lambda i, j: (0, 0)),
                  _resident((s, LANES))],
        out_specs=pl.BlockSpec((Q_BLOCK, ATT_W), lambda i, j: (i * nqb + j, 0)),
        out_shape=jax.ShapeDtypeStruct((b * s, ATT_W), MXU),
        scratch_shapes=[pltpu.VMEM((2 * LANES, N_HEADS * Q_BLOCK), MXU), pltpu.VMEM((n_sel, N_KV * Q_BLOCK), F32)],
        compiler_params=_cparams(est, 2),
        name="attention_prompt",
    )(qt, kc, vct, krows, va, gt, cov, onehot)


def _attn_sample_kernel(pt_ref, *refs, n_pages, ns, n_new, pos0, n_sel, win_past):
    del pt_ref
    page_refs = [refs[j * n_pages:(j + 1) * n_pages] for j in range(ns)]
    zrow_ref, wnewt_ref, wpast_ref, kc_ref, vc_ref, cov_ref, exp_ref = refs[ns * n_pages:ns * n_pages + 7]
    o_ref, wout_ref = refs[-2:]
    n = n_pages * (PAGE // CMP_STRIDE)
    past = n_pages * PAGE
    grp = Q_PER_KV * SUBLANES
    per_seq = N_KV * grp
    rows = ns * per_seq
    g_off = ATT_W + 6 * KV_W

    lane_h = lax.broadcasted_iota(jnp.int32, (1, KV_W), 1)
    head_lanes = [(lane_h >= g * HEAD_DIM) & (lane_h < (g + 1) * HEAD_DIM) for g in range(N_KV)]
    zrows = [zrow_ref[j] for j in range(ns)]

    def per_row(fn):
        return jnp.concatenate([fn(j, g, r) for j in range(ns) for g in range(N_KV) for r in range(Q_PER_KV)], axis=0)

    def per_seq_rows(fn):
        return jnp.concatenate([jnp.broadcast_to(fn(j), (per_seq, fn(j).shape[1])) for j in range(ns)], axis=0)

    def seq_matmul(fn):
        return jnp.concatenate([fn(j, slice(j * per_seq, (j + 1) * per_seq)) for j in range(ns)], axis=0)

    qf = per_row(lambda j, g, r: jnp.where(head_lanes[g], zrows[j][:, r * LANES:(r + 1) * LANES], 0.0))
    q = qf.astype(MXU)
    gate = [per_row(lambda j, g, r, k=k: zrows[j][:, g_off + k * N_HEADS + g * Q_PER_KV + r:
                                                  g_off + k * N_HEADS + g * Q_PER_KV + r + 1]) for k in range(3)]

    step = jnp.bitwise_and(lax.broadcasted_iota(jnp.int32, (rows, 1), 0), SUBLANES - 1)
    qp = pos0 + step
    lane_c = lax.broadcasted_iota(jnp.int32, (1, n), 1)
    valid_c = lane_c * CMP_STRIDE + (CMP_LEN - 1) <= qp
    wl = lax.broadcasted_iota(jnp.int32, (1, win_past), 1)
    wpos = pos0 - win_past + wl
    valid_w = (wpos >= 0) & (wpos <= qp) & (wpos >= qp - WINDOW)

    sc = seq_matmul(lambda j, r: _dot_nt(q[r], kc_ref[j]))
    s = jnp.where(valid_c, sc, NEG)
    mx = jnp.max(s, axis=1, keepdims=True)
    mx = jnp.where(mx > 0.5 * NEG, mx, 0.0)
    e = jnp.where(valid_c, jnp.exp2(s - mx), 0.0)
    p = e / jnp.maximum(jnp.sum(e, axis=1, keepdims=True), 1.0)
    pm = p.astype(MXU)
    o_c = seq_matmul(lambda j, r: _dot(pm[r], vc_ref[j]))

    n_grp = ns * N_KV
    psum = jnp.concatenate([sum(p[gi * grp + r * SUBLANES:gi * grp + (r + 1) * SUBLANES] for r in range(Q_PER_KV))
                            for gi in range(n_grp)], axis=0)
    imp = _dot3(psum, cov_ref[...])
    jl = lax.broadcasted_iota(jnp.int32, (1, LANES), 1)
    qp8 = pos0 + jnp.bitwise_and(lax.broadcasted_iota(jnp.int32, (n_grp * SUBLANES, 1), 0), SUBLANES - 1)
    cur = lax.shift_right_logical(qp8, 6)
    forced = (jl == 0) | ((jl <= cur) & (jl > cur - N_LOCAL))
    causal = (jl * SEL_BLOCK <= qp8) & (jl < n_sel)
    score = jnp.where(forced, FORCED_SCORE, jnp.where(causal, imp, -1.0))
    cnt = jnp.zeros_like(score)
    for jp in range(n_sel):
        col = score[:, jp:jp + 1]
        cnt = cnt + jnp.where((col > score) | ((col == score) & (jp < jl)), 1.0, 0.0)
    sel8 = ((cnt < float(SEL_TOPK)) & causal).astype(F32)
    sel = jnp.concatenate([sel8[gi * SUBLANES:(gi + 1) * SUBLANES] for gi in range(n_grp) for _ in range(Q_PER_KV)],
                          axis=0)
    keymask = _dot(sel.astype(MXU), exp_ref[...])

    def new_keys(lo, ok_fn):
        out = []
        for tk in range(n_new):
            krow = per_seq_rows(lambda j: zrows[j][tk:tk + 1, lo:lo + KV_W])
            vrow = per_seq_rows(lambda j: zrows[j][tk:tk + 1, lo + KV_W:lo + 2 * KV_W])
            sv = jnp.sum(qf * krow, axis=1, keepdims=True)
            out.append((jnp.where(ok_fn(tk), sv, NEG), vrow))
        return out

    def softmax_two(s_past, extra, v_of):
        mx = jnp.max(s_past, axis=1, keepdims=True)
        for sv, _ in extra:
            mx = jnp.maximum(mx, sv)
        pp = jnp.exp2(s_past - mx)
        den = jnp.sum(pp, axis=1, keepdims=True)
        pb = pp.astype(MXU)
        acc = seq_matmul(lambda j, r: _dot_nt(pb[r], v_of(j)))
        for sv, vv in extra:
            pe = jnp.exp2(sv - mx)
            den = den + pe
            acc = acc + pe * vv
        return acc / den

    def sel_rows(j, lo):
        return jnp.concatenate([pr[0, 0, lo:lo + KV_W, :] for pr in page_refs[j]], axis=1).astype(MXU)

    s_sel = jnp.where(keymask > 0.5, seq_matmul(lambda j, r: _dot(q[r], sel_rows(j, 0))), NEG)
    new_blk = past // SEL_BLOCK
    sel_new = sel[:, new_blk:new_blk + 1] > 0.5
    extra = new_keys(ATT_W + 2 * KV_W, lambda tk: sel_new & (tk <= step))
    o_s = softmax_two(s_sel, extra, lambda j: sel_rows(j, KV_W))

    wpast = [wpast_ref[0, j] for j in range(ns)]
    s_win = jnp.where(valid_w, seq_matmul(lambda j, r: _dot(q[r], wpast[j][0:KV_W].astype(MXU))), NEG)
    extra = new_keys(ATT_W + 4 * KV_W, lambda tk: tk <= step)
    o_w = softmax_two(s_win, extra, lambda j: wpast[j][KV_W:2 * KV_W].astype(MXU))

    o = o_c * gate[0] + o_s * gate[1] + o_w * gate[2]
    for j in range(ns):
        base = j * per_seq
        o_ref[j] = jnp.where(head_lanes[0], o[base:base + grp], o[base + grp:base + 2 * grp])
        rolled = pltpu.roll(wpast[j], win_past - n_new, 1)
        newr = pltpu.roll(wnewt_ref[j], LANES - n_new, 1)
        wout_ref[0, j, :, 0:win_past - LANES] = rolled[:, 0:win_past - LANES]
        wout_ref[0, j, :, win_past - LANES:win_past] = jnp.where(jl >= LANES - n_new, newr,
                                                                 rolled[:, win_past - LANES:win_past])


def _attention_sample(page_table, cache_t, zrow, wnewt, wpast_t, kc, vc, cov, expand, win_prev,
                      *, layer, ns, pos0, n_new):
    bd, n_pages = page_table.shape
    n = n_pages * (PAGE // CMP_STRIDE)
    past = n_pages * PAGE
    n_sel = -(-(past + n_new) // SEL_BLOCK)
    win_past = wpast_t.shape[3]
    rows = Q_PER_KV * SUBLANES

    in_specs = ([pl.BlockSpec((1, 1, 2 * KV_W, PAGE), lambda i, pt, j=j, k=k: (layer, pt[i * ns + j, k], 1, 0))
                 for j in range(ns) for k in range(n_pages)]
                + [pl.BlockSpec((ns, SUBLANES, Z_ALL), lambda i, pt: (i, 0, 0)),
                   pl.BlockSpec((ns, 2 * KV_W, LANES), lambda i, pt: (i, 0, 0)),
                   pl.BlockSpec((1, ns, 2 * KV_W, win_past), lambda i, pt: (layer, i, 0, 0)),
                   pl.BlockSpec((ns, n, LANES), lambda i, pt: (i, 0, 0)),
                   pl.BlockSpec((ns, n, LANES), lambda i, pt: (i, 0, 0)),
                   pl.BlockSpec((n, LANES), lambda i, pt: (0, 0)),
                   pl.BlockSpec((LANES, past), lambda i, pt: (0, 0))])
    est = (2 * ns * n_pages * 2 * KV_W * PAGE * 4 + ns * 8 * past * KV_W * 4 + 8 * ns * 2 * KV_W * win_past * 4
           + (6 << 20))
    args = [page_table, *([cache_t] * (ns * n_pages)), zrow, wnewt, wpast_t, kc, vc, cov, expand]
    aliases = {}
    if win_prev is not None:
        aliases = {len(args): 1}
        in_specs.append(pl.BlockSpec(memory_space=pl.ANY))
        args.append(win_prev)
    return pl.pallas_call(
        functools.partial(_attn_sample_kernel, n_pages=n_pages, ns=ns, n_new=n_new, pos0=pos0, n_sel=n_sel,
                          win_past=win_past),
        grid_spec=pltpu.PrefetchScalarGridSpec(
            num_scalar_prefetch=1,
            grid=(bd // ns,),
            in_specs=in_specs,
            out_specs=(pl.BlockSpec((ns, rows, KV_W), lambda i, pt: (i, 0, 0)),
                       pl.BlockSpec((1, ns, 2 * KV_W, win_past), lambda i, pt: (layer, i, 0, 0)))),
        out_shape=(jax.ShapeDtypeStruct((bd, rows, KV_W), F32),
                   jax.ShapeDtypeStruct(wpast_t.shape, F32)),
        input_output_aliases=aliases,
        compiler_params=_cparams(est),
        name="attention_sample",
    )(*args)


def _merge_kernel(x_ref, yp_ref, yn_ref, yc_ref, sg_ref, wbp_ref, wbn_ref, wbc_ref, wo_ref, o_ref):
    d = D_MODEL
    merged = (sg_ref[:, 0:d].astype(F32) * _dot(yp_ref[...], wbp_ref[...])
              + sg_ref[:, d:2 * d].astype(F32) * _dot(yn_ref[...], wbn_ref[...])
              + sg_ref[:, 2 * d:3 * d].astype(F32) * _dot(yc_ref[...], wbc_ref[...]))
    o_ref[...] = x_ref[...] + _dot(merged.astype(MXU), wo_ref[...])


def _merge(x, yp, yn, yc, sg, wbp, wbn, wbc, wo, *, tm):
    m = x.shape[0]
    row = lambda w: pl.BlockSpec((tm, w), lambda i: (i, 0))
    est = 2 * tm * (2 * D_MODEL * 4 + 1024 * 2 + 3072 * 2) + 2 * 3 * D_MODEL * D_MODEL + 6 * tm * D_MODEL * 4 + (2 << 20)
    return pl.pallas_call(
        _merge_kernel,
        grid=(m // tm,),
        in_specs=[row(D_MODEL), row(POOL_W), row(ATT_W), row(CONV_DIM), row(3 * D_MODEL),
                  _resident((POOL_W, D_MODEL)), _resident((ATT_W, D_MODEL)), _resident((CONV_DIM, D_MODEL)),
                  _resident((D_MODEL, D_MODEL))],
        out_specs=row(D_MODEL),
        out_shape=jax.ShapeDtypeStruct((m, D_MODEL), F32),
        compiler_params=_cparams(est),
        name="merge",
    )(x, yp, yn, yc, sg, wbp, wbn, wbc, wo)


def _ffn_kernel(x_ref, g_ref, pre_ref, wup_ref, wcv_ref, wdn_ref, gf_ref, o_ref, last_ref, *rest,
                tm, stride, tiles_per_seq, final):
    yf_ref = rest[0] if final else None
    aext = rest[-1]
    i = pl.program_id(0)
    hr = aext.shape[0] - tm

    @pl.when(i % tiles_per_seq == 0)
    def _():
        aext[0:hr, :] = pre_ref[...]

    x = x_ref[...]
    h = x * lax.rsqrt(jnp.mean(x * x, axis=-1, keepdims=True) + EPS)
    h = (h * g_ref[...]).astype(MXU)
    acc = x
    for j in range(D_FF // FF_CHUNK):
        cj = slice(j * FF_CHUNK, (j + 1) * FF_CHUNK)
        a = _dot(h, wup_ref[:, cj])
        aext[hr:hr + tm, cj] = a
        a1 = aext[hr - stride:hr - stride + tm, cj]
        a2 = aext[hr - 2 * stride:hr - 2 * stride + tm, cj]
        ac = wcv_ref[0:1, cj] * a2 + wcv_ref[1:2, cj] * a1 + wcv_ref[2:3, cj] * a
        bgate = _dot(h, wup_ref[:, D_FF + j * FF_CHUNK:D_FF + (j + 1) * FF_CHUNK])
        gated = (ac * _sigmoid(ac) * bgate).astype(MXU)
        acc = acc + _dot(gated, wdn_ref[cj, :])
    o_ref[...] = acc
    if final:
        yf = acc * lax.rsqrt(jnp.mean(acc * acc, axis=-1, keepdims=True) + EPS)
        yf_ref[...] = yf * gf_ref[...]
    keep = (CONV_TAPS - 1) * stride
    last_ref[0] = aext[hr + tm - keep:hr + tm, :]
    if tiles_per_seq > 1:
        aext[0:hr, :] = aext[tm:tm + hr, :]


def _conv_ffn(x, g, pre, wup, wcv, wdn, gf, *, tm, stride, tiles_per_seq, final):
    m = x.shape[0]
    nt = m // tm
    nseq = nt // tiles_per_seq
    hr = pre.shape[0] // nseq
    keep = (CONV_TAPS - 1) * stride
    row = pl.BlockSpec((tm, D_MODEL), lambda i: (i, 0))
    est = (3 * 2 * tm * D_MODEL * 4 + wup.size * 2 + wdn.size * 2 + (hr + tm) * D_FF * 4 + 2 * hr * D_FF * 4
           + 2 * keep * D_FF * 4 + 8 * tm * D_MODEL * 4 + (2 << 20))
    out_specs = [row, pl.BlockSpec((1, keep, D_FF), lambda i: (i // tiles_per_seq, 0, 0))]
    out_shape = [jax.ShapeDtypeStruct((m, D_MODEL), F32), jax.ShapeDtypeStruct((nseq, keep, D_FF), F32)]
    if final:
        out_specs.append(row)
        out_shape.append(jax.ShapeDtypeStruct((m, D_MODEL), F32))
    res = pl.pallas_call(
        functools.partial(_ffn_kernel, tm=tm, stride=stride, tiles_per_seq=tiles_per_seq, final=final),
        grid=(nt,),
        in_specs=[row, pl.BlockSpec((1, D_MODEL), lambda i: (0, 0)),
                  pl.BlockSpec((hr, D_FF), lambda i: (i // tiles_per_seq, 0)),
                  _resident((D_MODEL, 2 * D_FF)), pl.BlockSpec((CONV_TAPS, D_FF), lambda i: (0, 0)),
                  _resident((D_FF, D_MODEL)), pl.BlockSpec((1, D_MODEL), lambda i: (0, 0))],
        out_specs=tuple(out_specs),
        out_shape=tuple(out_shape),
        scratch_shapes=[pltpu.VMEM((hr + tm, D_FF), F32)],
        compiler_params=_cparams(est),
        name="conv_ffn",
    )(x, g, pre, wup, wcv, wdn, gf)
    return (res[0], res[1], res[2] if final else None)


def _rot_cols(w):
    k = w.shape[0]
    w4 = w.reshape(k, -1, 2, HEAD_DIM // 2)
    return jnp.stack([-w4[:, :, 1], w4[:, :, 0]], axis=2).reshape(k, -1)


def _rope_table_t(pos):
    half = HEAD_DIM // 2
    inv = ROPE_THETA ** (-jnp.arange(half, dtype=F32) / half)
    ang = pos.astype(F32)[:, None] * inv[None, :]
    return jnp.concatenate([jnp.tile(jnp.cos(ang), (1, 4)), jnp.tile(jnp.sin(ang), (1, 4))], axis=1).T


def _blockdiag2(w):
    z = jnp.zeros_like(w)
    return jnp.concatenate([jnp.concatenate([w, z], axis=1), jnp.concatenate([z, w], axis=1)], axis=0)


def _head_major_to_pair_major(w, axis):
    shp = w.shape
    w = w.reshape(shp[:axis] + (N_KV, Q_PER_KV, HEAD_DIM) + shp[axis + 1:])
    w = jnp.swapaxes(w, axis, axis + 1)
    return w.reshape(shp)


def _layer_weights(w_in, pool_w, cmp_pe, cmp_w1, cmp_w2, w_br_nsa):
    parts = []
    off = 0
    for w in IN_SPLITS:
        parts.append(w_in[:, off:off + w])
        off += w
    u, q, kv, gate, cb, cc, cx, gp, gn, gc = parts
    kv6 = kv.reshape(D_MODEL, 6, KV_W)
    kcmp, vcmp, ksel, vsel, kwin, vwin = [kv6[:, s] for s in range(6)]
    wrow = jnp.concatenate([u, cb, cc, cx, gp, gn, gc], axis=1).astype(MXU)
    qs = _head_major_to_pair_major(q, 1) * (HEAD_DIM ** -0.5 * LOG2E)
    gpad = jnp.pad(gate, ((0, 0), (0, LANES - gate.shape[1])))
    wt = jnp.concatenate([qs, _rot_cols(qs), kcmp, vcmp, ksel, _rot_cols(ksel), vsel,
                          kwin, _rot_cols(kwin), vwin, gpad], axis=1).T.astype(MXU)
    wbn = _head_major_to_pair_major(w_br_nsa, 0).astype(MXU)

    n_grp = len(POOL_WINDOWS)
    same_grp = jnp.eye(n_grp, dtype=bool)[:, None, :, None]
    pw = jnp.where(same_grp, pool_w[:, :, None, :], 0.0).reshape(POOL_W, POOL_W)

    wab = []
    peab = []
    for s in range(2):
        w1r = cmp_w1[s].reshape(CMP_LEN, HEAD_DIM, CMP_HIDDEN)
        halves = []
        for hsel in range(2):
            wh = w1r[hsel * CMP_STRIDE:(hsel + 1) * CMP_STRIDE]
            same_head = jnp.eye(N_KV, dtype=bool)[None, :, None, :, None]
            z = jnp.where(same_head, wh[:, None, :, None, :], 0.0)
            halves.append(z.reshape(CMP_STRIDE * KV_W, N_KV * CMP_HIDDEN))
            pe = cmp_pe[s][hsel * CMP_STRIDE:(hsel + 1) * CMP_STRIDE]
            peab.append(jnp.broadcast_to(pe[:, None, :], (CMP_STRIDE, N_KV, HEAD_DIM)).reshape(1, -1))
        wab.append(jnp.stack(halves))
    wab = jnp.stack(wab).astype(MXU)
    peab = jnp.concatenate(peab + [jnp.zeros((SUBLANES - 4, CMP_STRIDE * KV_W), F32)], axis=0)
    w2k = jnp.concatenate([_blockdiag2(cmp_w2[0]), _blockdiag2(_rot_cols(cmp_w2[0]))], axis=1).astype(MXU)
    w2v = _blockdiag2(cmp_w2[1]).astype(MXU)
    return dict(wrow=wrow, wt=wt, wbn=wbn, pw=pw.astype(MXU), wab=wab, peab=peab, w2k=w2k, w2v=w2v, w2vt=w2v.T)


def _cover(n_cmp, n_sel):
    c_start = jnp.arange(n_cmp) * CMP_STRIDE
    j_start = jnp.arange(n_sel) * SEL_BLOCK
    return ((c_start[:, None] < j_start[None, :] + SEL_BLOCK)
            & (c_start[:, None] + CMP_LEN > j_start[None, :])).astype(MXU)


def _feature_major(x):
    nd = x.ndim
    perm = tuple(range(nd - 4)) + (nd - 3, nd - 2, nd - 1, nd - 4)
    xt = jnp.transpose(x, perm)
    return xt.reshape(xt.shape[:nd - 4] + (-1, xt.shape[-1]))


def _token_major(xt, a, b, c):
    nd = xt.ndim
    x = xt.reshape(xt.shape[:nd - 2] + (a, b, c, xt.shape[-1]))
    perm = tuple(range(nd - 2)) + (nd + 1, nd - 2, nd - 1, nd)
    return jnp.transpose(x, perm)


def kernel(x_prompt, x_sample, cache_kv, cache_win, state_pool, state_conv, state_ffn, page_table,
           norm_mix, w_in, pool_w, pool_scale, cmp_pe, cmp_w1, cmp_w2, conv_w,
           w_br_pool, w_br_nsa, w_br_conv, w_out, norm_ffn, ffn_up, ffn_conv, ffn_down, norm_final):
    bp, sp, d = x_prompt.shape
    bd, sd, _ = x_sample.shape
    depth = w_in.shape[0]
    n_pages = page_table.shape[1]
    past = n_pages * PAGE
    wpast = cache_win.shape[2]
    assert d == D_MODEL and sp % (4 * Q_BLOCK) == 0 and sp >= WINDOW and sp // SEL_BLOCK <= LANES
    assert bd == LANES and sd & (sd - 1) == 0 and sd <= SUBLANES
    assert past % SEL_BLOCK == 0 and cache_kv.shape[2] == PAGE and wpast == WINDOW

    tm_p = 512
    mp = bp * sp
    md = bd * sd
    tps = sp // tm_p

    cst_p = _rope_table_t(jnp.arange(sp))
    cst_d = _rope_table_t(jnp.repeat(past + jnp.arange(sd), bd))
    nc_p = sp // CMP_STRIDE
    nc_d = past // CMP_STRIDE
    csc_p = _rope_table_t(jnp.arange(nc_p) * CMP_STRIDE + CMP_LEN - 1).T
    csc_d = _rope_table_t(jnp.arange(nc_d) * CMP_STRIDE + CMP_LEN - 1).T
    nsel_p = sp // SEL_BLOCK
    cov_pt = _cover(nc_p, nsel_p).T
    cov_d = _cover(nc_d, LANES)
    expand = (jnp.arange(LANES)[:, None] == (jnp.arange(past) // SEL_BLOCK)[None, :]).astype(MXU)
    onehot_p = ((jnp.arange(sp) // SEL_BLOCK)[:, None] == jnp.arange(LANES)[None, :]).astype(MXU)
    tok = jnp.arange(PAGE)
    perm = (tok[None, :] == (tok[:, None] % (PAGE // CMP_STRIDE)) * CMP_STRIDE + tok[:, None] // (PAGE // CMP_STRIDE))
    perm = perm.astype(MXU)

    hr_pool_p = 2 * SUBLANES
    hr_conv_p = SUBLANES
    zeros_pool = jnp.zeros((bp * hr_pool_p, POOL_W), F32)
    zeros_conv = jnp.zeros((bp * hr_conv_p, CONV_DIM), F32)
    zeros_ffn = jnp.zeros((bp * hr_conv_p, D_FF), F32)

    cache_t = _feature_major(cache_kv)
    win_t = _feature_major(cache_win)
    pool_tm = jnp.transpose(state_pool, (0, 2, 1, 3))

    xp = x_prompt.reshape(mp, d)
    xs = jnp.transpose(x_sample, (1, 0, 2)).reshape(md, d)

    keys = ("kv_s", "win_p", "pool_p", "pool_s", "conv_p", "conv_s", "ffn_p", "ffn_s")
    outs = {k: [] for k in keys}
    yp_final = ys_final = None
    kvt = win_new = None
    for l in range(depth):
        lw = _layer_weights(w_in[l], pool_w[l], cmp_pe[l], cmp_w1[l], cmp_w2[l], w_br_nsa[l])
        g_mix = norm_mix[l].reshape(1, d)
        g_ffn = norm_ffn[l].reshape(1, d)
        ps = pool_scale[l].reshape(1, POOL_W)
        cw = conv_w[l]
        wbp = w_br_pool[l].astype(MXU)
        wbc = w_br_conv[l].astype(MXU)
        wo = w_out[l].astype(MXU)
        wup = ffn_up[l].astype(MXU)
        wdn = ffn_down[l].astype(MXU)
        wcv = ffn_conv[l]
        gf = norm_final.reshape(1, d)
        final = l == depth - 1

        up, cb, ccx, sg, kvt, wint, qt, va, krows, gt = _in_proj(xp, g_mix, lw["wrow"], lw["wt"], cst_p,
                                                               tm=tm_p, seq=sp, sample=False, layer=l, depth=depth,
                                                               kv_prev=kvt)
        ypool, yconv = _mixers(up, cb, ccx, zeros_pool, zeros_conv, lw["pw"], ps, cw,
                               tm=tm_p, stride=1, tiles_per_seq=tps, pos0=0)
        kc, vct = _compress_prompt(kvt, perm, lw["peab"], lw["wab"], lw["w2k"], lw["w2vt"], csc_p, layer=l)
        ynsa = _attention_prompt(qt, kc, vct, krows, va, gt, cov_pt, onehot_p, b=bp, s=sp)
        x1 = _merge(xp, ypool, ynsa, yconv, sg, wbp, lw["wbn"], wbc, wo, tm=tm_p)
        xp, a_last, yp_final = _conv_ffn(x1, g_ffn, zeros_ffn, wup, wcv, wdn, gf, tm=tm_p, stride=1,
                                         tiles_per_seq=tps, final=final)
        outs["win_p"].append(wint[:, :, sp - WINDOW:])
        outs["pool_p"].append(up.reshape(bp, sp, POOL_W)[:, sp - (POOL_MAX - 1):])
        outs["conv_p"].append(ccx.reshape(bp, sp, CONV_DIM)[:, sp - (CONV_TAPS - 1):])
        outs["ffn_p"].append(a_last)

        up, cb, ccx, sg, zall = _in_proj(xs, g_mix, lw["wrow"], lw["wt"], cst_d, tm=md, seq=sd, sample=True)
        pool_pre = pool_tm[l].reshape((POOL_MAX - 1) * bd, POOL_W)
        conv_pre = jnp.transpose(state_conv[l], (1, 0, 2)).reshape((CONV_TAPS - 1) * bd, CONV_DIM)
        ffn_pre = jnp.transpose(state_ffn[l], (1, 0, 2)).reshape((CONV_TAPS - 1) * bd, D_FF)
        ypool, yconv = _mixers(up, cb, ccx, pool_pre, conv_pre, lw["pw"], ps, cw,
                               tm=md, stride=bd, tiles_per_seq=1, pos0=past)
        zrow = jnp.pad(jnp.transpose(zall, (2, 0, 1)), ((0, 0), (0, SUBLANES - sd), (0, 0)))
        wnew_t = zall[:, ATT_W + 4 * KV_W:ATT_W + 6 * KV_W, :]
        wnew_tp = jnp.pad(jnp.transpose(wnew_t, (2, 1, 0)), ((0, 0), (0, 0), (0, LANES - sd)))
        kc_d, vc_d = _compress_sample(page_table, cache_t, perm, lw["peab"], lw["wab"], lw["w2k"], lw["w2v"], csc_d,
                                      layer=l, nb=4)
        o_seq, win_new = _attention_sample(page_table, cache_t, zrow, wnew_tp, win_t, kc_d, vc_d, cov_d, expand,
                                           win_new, layer=l, ns=4, pos0=past, n_new=sd)
        ynsa = jnp.transpose(o_seq.reshape(bd, Q_PER_KV, SUBLANES, KV_W)[:, :, 0:sd], (2, 0, 1, 3))
        ynsa = ynsa.reshape(md, ATT_W).astype(MXU)
        x1 = _merge(xs, ypool, ynsa, yconv, sg, wbp, lw["wbn"], wbc, wo, tm=md)
        xs, a_last, ys_final = _conv_ffn(x1, g_ffn, ffn_pre, wup, wcv, wdn, gf, tm=md, stride=bd,
                                         tiles_per_seq=1, final=final)
        outs["kv_s"].append(zall[:, ATT_W:ATT_W + 4 * KV_W, :])
        up_tm = up.reshape(sd, bd, POOL_W)
        pool_full = jnp.concatenate([pool_tm[l], up_tm], axis=0)
        outs["pool_s"].append(pool_full[pool_full.shape[0] - (POOL_MAX - 1):])
        ccx_seq = jnp.transpose(ccx.reshape(sd, bd, CONV_DIM), (1, 0, 2))
        conv_full = jnp.concatenate([state_conv[l], ccx_seq], axis=1)
        outs["conv_s"].append(conv_full[:, conv_full.shape[1] - (CONV_TAPS - 1):])
        outs["ffn_s"].append(jnp.transpose(a_last.reshape(CONV_TAPS - 1, bd, D_FF), (1, 0, 2)))

    st = lambda k: jnp.stack(outs[k])
    y_prompt = yp_final.reshape(bp, sp, d)
    y_sample = jnp.transpose(ys_final.reshape(sd, bd, d), (1, 0, 2))
    kv_prompt = _token_major(kvt, 4, N_KV, HEAD_DIM)
    kv_s = st("kv_s").reshape(depth, sd, 4, N_KV, HEAD_DIM, bd)
    kv_sample = jnp.transpose(kv_s, (0, 5, 1, 2, 3, 4))
    win_prompt = _token_major(st("win_p"), 2, N_KV, HEAD_DIM)
    win_sample = _token_major(win_new, 2, N_KV, HEAD_DIM)
    pool_sample = jnp.transpose(st("pool_s"), (0, 2, 1, 3))
    return (y_prompt, y_sample, kv_prompt, kv_sample, win_prompt, win_sample, st("pool_p"), pool_sample,
            st("conv_p"), st("conv_s"), st("ffn_p"), st("ffn_s"))
```

```python
import functools
import math

import jax
import jax.numpy as jnp
from jax import lax
from jax.experimental import pallas as pl
from jax.experimental.pallas import tpu as pltpu

D_MODEL = 1024
HEAD_DIM = 64
N_HEADS = 8
N_KV = 2
Q_PER_KV = 4
ATT_W = N_HEADS * HEAD_DIM
KV_W = N_KV * HEAD_DIM
CMP_LEN = 32
CMP_STRIDE = 16
CMP_HIDDEN = 128
SEL_BLOCK = 64
SEL_TOPK = 16
N_LOCAL = 2
WINDOW = 512
Q_BLOCK = 128
FORCED_SCORE = 1e4
POOL_WINDOWS = (2, 4, 8, 16)
POOL_W = 256
POOL_MAX = 16
CONV_TAPS = 3
CONV_DIM = 256
D_FF = 2816
ROPE_THETA = 10000.0
EPS = 1e-6
PAGE = 128
IN_SPLITS = (POOL_W, ATT_W, 6 * KV_W, 3 * N_HEADS, CONV_DIM, CONV_DIM, CONV_DIM, D_MODEL, D_MODEL, D_MODEL)

LANES = 128
SUBLANES = 8
BF16_ROWS = 16
V7X_VMEM_BYTES = 64 * 1024 * 1024
VMEM_CAP = 56 * 1024 * 1024

NEG = -1e30
MXU = jnp.bfloat16
F32 = jnp.float32
LOG2E = math.log2(math.e)

W_ROW = POOL_W + 3 * CONV_DIM + 3 * D_MODEL
T_Q = 2 * ATT_W
T_KV = 5 * KV_W
T_WIN = 3 * KV_W
W_T = T_Q + T_KV + T_WIN + LANES
VA_ROWS = HEAD_DIM + BF16_ROWS
Z_ALL = ATT_W + 4 * KV_W + 2 * KV_W + LANES
FF_CHUNK = 256


def _cparams(est_bytes, ndim=1):
    limit = int(min(max(est_bytes, 16 * 1024 * 1024), VMEM_CAP))
    return pltpu.CompilerParams(dimension_semantics=("arbitrary",) * ndim, vmem_limit_bytes=limit)


def _resident(shape):
    n = len(shape)
    return pl.BlockSpec(shape, lambda *_: (0,) * n, pipeline_mode=pl.Buffered(1))


def _sigmoid(x):
    return 1.0 / (1.0 + jnp.exp(-x))


def _dot(a, b):
    return jnp.dot(a, b, preferred_element_type=F32)


def _dot_nt(a, b):
    return lax.dot_general(a, b, (((1,), (1,)), ((), ())), preferred_element_type=F32)


def _split3(x):
    hi = x.astype(MXU)
    r1 = x - hi.astype(F32)
    mid = r1.astype(MXU)
    lo = (r1 - mid.astype(F32)).astype(MXU)
    return hi, mid, lo


def _dot3(x, w):
    out = None
    for term in _split3(x):
        t = _dot(term, w)
        out = t if out is None else out + t
    return out


def _inproj_kernel(x_ref, g_ref, wrow_ref, wt_ref, cst_ref, *rest, tm, sample, has_prev):
    up_ref, cb_ref, ccx_ref, sg_ref, *outs = rest[1:] if has_prev else rest
    x = x_ref[...]
    h = x * lax.rsqrt(jnp.mean(x * x, axis=-1, keepdims=True) + EPS)
    h = (h * g_ref[...]).astype(MXU)

    up_ref[...] = _dot(h, wrow_ref[:, 0:POOL_W])
    z = _dot(h, wrow_ref[:, POOL_W:POOL_W + 3 * CONV_DIM])
    cb_ref[...] = z[:, 0:256]
    ccx_ref[...] = z[:, 256:512] * z[:, 512:768]
    g0 = POOL_W + 3 * CONV_DIM
    for j in range(6):
        z = _dot(h, wrow_ref[:, g0 + 512 * j:g0 + 512 * (j + 1)])
        sg_ref[:, 512 * j:512 * (j + 1)] = _sigmoid(z).astype(sg_ref.dtype)

    cost = cst_ref[0:LANES, :]
    sint = cst_ref[LANES:2 * LANES, :]
    zq = _dot_nt(wt_ref[0:T_Q, :], h)
    q = jnp.concatenate(
        [zq[i * LANES:(i + 1) * LANES] * cost + zq[ATT_W + i * LANES:ATT_W + (i + 1) * LANES] * sint
         for i in range(ATT_W // LANES)], axis=0)
    zk = _dot_nt(wt_ref[T_Q:T_Q + T_KV, :], h)
    ksel = zk[256:384] * cost + zk[384:512] * sint
    kv = jnp.concatenate([zk[0:256], ksel, zk[512:640]], axis=0)
    zw = _dot_nt(wt_ref[T_Q + T_KV:T_Q + T_KV + T_WIN, :], h)
    kwin = zw[0:128] * cost + zw[128:256] * sint
    win = jnp.concatenate([kwin, zw[256:384]], axis=0)
    gates = _sigmoid(_dot_nt(wt_ref[T_Q + T_KV + T_WIN:W_T, :], h))

    if sample:
        (zall_ref,) = outs
        for c in range(tm // LANES):
            cols = slice(c * LANES, (c + 1) * LANES)
            zall_ref[c] = jnp.concatenate([q[:, cols], kv[:, cols], win[:, cols], gates[:, cols]], axis=0)
    else:
        kvt_ref, wint_ref, qt_ref, va_ref, krows_ref, gt_ref = outs
        kvt_ref[0, 0] = kv
        wint_ref[0] = win
        ones = jnp.ones((BF16_ROWS, LANES), va_ref.dtype)
        vsel = kv[384:512].astype(va_ref.dtype)
        vwin = win[128:256].astype(va_ref.dtype)
        for c in range(tm // LANES):
            cols = slice(c * LANES, (c + 1) * LANES)
            qt_ref[c] = q[:, cols].astype(qt_ref.dtype)
            va_ref[c] = jnp.concatenate(
                [vsel[0:64, cols], ones, vsel[64:128, cols], ones, vwin[0:64, cols], ones, vwin[64:128, cols], ones],
                axis=0)
            gt_ref[c] = gates[:, cols]
        krows_ref[:, 0:KV_W] = ksel.T.astype(krows_ref.dtype)
        krows_ref[:, KV_W:2 * KV_W] = kwin.T.astype(krows_ref.dtype)


def _in_proj(x, g, wrow, wt, cst, *, tm, seq, sample, layer=0, depth=1, kv_prev=None):
    m = x.shape[0]
    nt = m // tm
    ntab = cst.shape[1] // tm
    nb = tm // LANES
    tps = seq // tm if not sample else 1
    row = lambda w: pl.BlockSpec((tm, w), lambda i: (i, 0))
    blk = lambda r: pl.BlockSpec((nb, r, LANES), lambda i: (i, 0, 0))
    out_shape = [jax.ShapeDtypeStruct((m, POOL_W), F32), jax.ShapeDtypeStruct((m, CONV_DIM), F32),
                 jax.ShapeDtypeStruct((m, CONV_DIM), F32), jax.ShapeDtypeStruct((m, 3 * D_MODEL), MXU)]
    out_specs = [row(POOL_W), row(CONV_DIM), row(CONV_DIM), row(3 * D_MODEL)]
    if sample:
        out_shape += [jax.ShapeDtypeStruct((m // LANES, Z_ALL, LANES), F32)]
        out_specs += [blk(Z_ALL)]
    else:
        out_shape += [jax.ShapeDtypeStruct((depth, m // seq, 4 * KV_W, seq), F32),
                      jax.ShapeDtypeStruct((m // seq, 2 * KV_W, seq), F32),
                      jax.ShapeDtypeStruct((m // LANES, ATT_W, LANES), MXU),
                      jax.ShapeDtypeStruct((m // LANES, 4 * VA_ROWS, LANES), MXU),
                      jax.ShapeDtypeStruct((m, 2 * KV_W), MXU),
                      jax.ShapeDtypeStruct((m // LANES, LANES, LANES), F32)]
        out_specs += [pl.BlockSpec((1, 1, 4 * KV_W, tm), lambda i: (layer, i // tps, 0, i % tps)),
                      pl.BlockSpec((1, 2 * KV_W, tm), lambda i: (i // tps, 0, i % tps)),
                      blk(ATT_W), blk(4 * VA_ROWS), row(2 * KV_W), blk(LANES)]
    est = (2 * tm * D_MODEL * 4 + wrow.size * 2 + wt.size * 2 + 4 * tm * 256 * 4
           + 2 * tm * (3 * 256 * 4 + 3072 * 2) + 2 * tm * Z_ALL * 4 + tm * (T_Q + T_KV + T_WIN + 1024 + 1024) * 4
           + (4 << 20))
    in_specs = [pl.BlockSpec((tm, D_MODEL), lambda i: (i, 0)),
                pl.BlockSpec((1, D_MODEL), lambda i: (0, 0)),
                _resident((D_MODEL, W_ROW)),
                _resident((W_T, D_MODEL)),
                pl.BlockSpec((2 * LANES, tm), lambda i: (0, i % ntab))]
    args = [x, g, wrow, wt, cst]
    aliases = {}
    if kv_prev is not None:
        aliases = {len(args): 4}
        in_specs.append(pl.BlockSpec(memory_space=pl.ANY))
        args.append(kv_prev)
    return pl.pallas_call(
        functools.partial(_inproj_kernel, tm=tm, sample=sample, has_prev=kv_prev is not None),
        grid=(nt,),
        in_specs=in_specs,
        out_specs=tuple(out_specs),
        out_shape=tuple(out_shape),
        input_output_aliases=aliases,
        compiler_params=_cparams(est),
        name="in_proj",
    )(*args)


def _mix_kernel(up_ref, cb_ref, ccx_ref, ppre_ref, cpre_ref, pw_ref, ps_ref, cw_ref,
                yp_ref, yc_ref, pext, cext, *, tm, stride, tiles_per_seq, pos0):
    i = pl.program_id(0)
    hp = pext.shape[0] - tm
    hc = cext.shape[0] - tm

    @pl.when(i % tiles_per_seq == 0)
    def _():
        pext[0:hp, :] = ppre_ref[...]
        cext[0:hc, :] = cpre_ref[...]

    u = up_ref[...]
    pext[hp:hp + tm, :] = u
    acc = u
    sums = {}
    for k in range(1, POOL_MAX):
        acc = acc + pext[hp - k * stride:hp - k * stride + tm, :]
        if k + 1 in POOL_WINDOWS:
            sums[k + 1] = acc
    row = lax.broadcasted_iota(jnp.int32, (tm, 1), 0)
    if stride == 1:
        t_abs = pos0 + (i % tiles_per_seq) * tm + row
    else:
        t_abs = pos0 + lax.shift_right_logical(row, stride.bit_length() - 1)
    tp1 = (t_abs + 1).astype(F32)
    lane = lax.broadcasted_iota(jnp.int32, (1, POOL_W), 1)
    grp = POOL_W // len(POOL_WINDOWS)
    mean = None
    for gi, w in reversed(list(enumerate(POOL_WINDOWS))):
        mw = sums[w] / jnp.minimum(float(w), tp1)
        mean = mw if mean is None else jnp.where(lane < (gi + 1) * grp, mw, mean)
    pooled = mean - u
    y = _dot(pooled.astype(MXU), pw_ref[...]) * ps_ref[...]
    yp_ref[...] = y.astype(yp_ref.dtype)

    e0 = ccx_ref[...]
    cext[hc:hc + tm, :] = e0
    e1 = cext[hc - stride:hc - stride + tm, :]
    e2 = cext[hc - 2 * stride:hc - 2 * stride + tm, :]
    conv = cw_ref[0:1, :] * e2 + cw_ref[1:2, :] * e1 + cw_ref[2:3, :] * e0
    yc_ref[...] = (cb_ref[...] * conv).astype(yc_ref.dtype)

    if tiles_per_seq > 1:
        pext[0:hp, :] = pext[tm:tm + hp, :]
        cext[0:hc, :] = cext[tm:tm + hc, :]


def _mixers(up, cb, ccx, ppre, cpre, pw, ps, cw, *, tm, stride, tiles_per_seq, pos0):
    m = up.shape[0]
    nt = m // tm
    hp = ppre.shape[0] // (nt // tiles_per_seq)
    hc = cpre.shape[0] // (nt // tiles_per_seq)
    row = pl.BlockSpec((tm, 256), lambda i: (i, 0))
    est = 2 * 5 * tm * 256 * 4 + (2 * tm + hp + hc) * 256 * 4 + 2 * (hp + hc) * 256 * 4 + 12 * tm * 256 * 4 + (2 << 20)
    return pl.pallas_call(
        functools.partial(_mix_kernel, tm=tm, stride=stride, tiles_per_seq=tiles_per_seq, pos0=pos0),
        grid=(nt,),
        in_specs=[row, row, row,
                  pl.BlockSpec((hp, 256), lambda i: (i // tiles_per_seq, 0)),
                  pl.BlockSpec((hc, 256), lambda i: (i // tiles_per_seq, 0)),
                  pl.BlockSpec((256, 256), lambda i: (0, 0)),
                  pl.BlockSpec((1, 256), lambda i: (0, 0)),
                  pl.BlockSpec((CONV_TAPS, 256), lambda i: (0, 0))],
        out_specs=(row, row),
        out_shape=(jax.ShapeDtypeStruct((m, 256), MXU), jax.ShapeDtypeStruct((m, 256), MXU)),
        scratch_shapes=[pltpu.VMEM((hp + tm, 256), F32), pltpu.VMEM((hc + tm, 256), F32)],
        compiler_params=_cparams(est),
        name="mixers",
    )(up, cb, ccx, ppre, cpre, pw, ps, cw)


def _gelu_tanh(x):
    return x * (0.5 * (1.0 + jnp.tanh(0.7978845608028654 * (x + 0.044715 * (x * x * x)))))


def _compress_bias(peab_ref, wab_ref):
    pe3 = _split3(peab_ref[...])
    out = []
    for s in range(2):
        ba = None
        bb = None
        for term in pe3:
            ta = _dot(term, wab_ref[s, 0])
            tb = _dot(term, wab_ref[s, 1])
            ba = ta if ba is None else ba + ta
            bb = tb if bb is None else bb + tb
        out.append(ba[2 * s:2 * s + 1, :] + bb[2 * s + 1:2 * s + 2, :])
    return out


def _compress_hidden(get_page, n_pages, perm_ref, wab_ref, bias, pb_scr):
    n = n_pages * (PAGE // CMP_STRIDE)
    per = PAGE // CMP_STRIDE
    ys = ([], [])
    for p in range(n_pages):
        yp = _dot_nt(perm_ref[...], get_page(p).astype(MXU))
        for s in range(2):
            ys[s].append(jnp.concatenate(
                [yp[t * per:(t + 1) * per, s * KV_W:(s + 1) * KV_W] for t in range(CMP_STRIDE)], axis=1))
    acts = []
    for s in range(2):
        rows = jnp.concatenate(ys[s], axis=0).astype(MXU)
        pa = _dot(rows, wab_ref[s, 0])
        pb = _dot(rows, wab_ref[s, 1])
        pb_scr[0:n, :] = pb
        pb_scr[n:n + SUBLANES, :] = jnp.zeros((SUBLANES, 2 * CMP_HIDDEN), F32)
        pre = pa + pb_scr[1:n + 1, :] + bias[s]
        acts.append(_gelu_tanh(pre).astype(MXU))
    return acts


def _cmp_prompt_kernel(kvt_ref, perm_ref, peab_ref, wab_ref, w2k_ref, w2vt_ref, csc_ref, kc_ref, vct_ref, pb_scr,
                       *, n_pages):
    def get_page(p):
        return kvt_ref[0, 0, :, p * PAGE:(p + 1) * PAGE]

    bias = _compress_bias(peab_ref, wab_ref)
    act_k, act_v = _compress_hidden(get_page, n_pages, perm_ref, wab_ref, bias, pb_scr)
    yk = _dot(act_k, w2k_ref[...])
    kc = yk[:, 0:LANES] * csc_ref[:, 0:LANES] + yk[:, LANES:2 * LANES] * csc_ref[:, LANES:2 * LANES]
    kc_ref[0] = kc.astype(kc_ref.dtype)
    vct_ref[0] = _dot_nt(w2vt_ref[...], act_v).astype(vct_ref.dtype)


def _cmp_sample_kernel(pt_ref, *refs, n_pages, nb):
    del pt_ref
    page_refs = refs[0:nb * n_pages]
    perm_ref, peab_ref, wab_ref, w2k_ref, w2v_ref, csc_ref, kc_ref, vc_ref, pb_scr, bias_scr = refs[nb * n_pages:]
    n = n_pages * (PAGE // CMP_STRIDE)

    @pl.when(pl.program_id(0) == 0)
    def _():
        for s, b in enumerate(_compress_bias(peab_ref, wab_ref)):
            bias_scr[s] = jnp.broadcast_to(b, (SUBLANES, 2 * CMP_HIDDEN))

    bias = [bias_scr[s, 0:1, :] for s in range(2)]
    act_k, act_v = _compress_hidden(lambda p: page_refs[p][0, 0], nb * n_pages, perm_ref, wab_ref, bias, pb_scr)
    yk = _dot(act_k, w2k_ref[...])
    vc = _dot(act_v, w2v_ref[...])
    for j in range(nb):
        r = slice(j * n, (j + 1) * n)
        kc = yk[r, 0:LANES] * csc_ref[:, 0:LANES] + yk[r, LANES:2 * LANES] * csc_ref[:, LANES:2 * LANES]
        kc_ref[j] = kc.astype(kc_ref.dtype)
        vc_ref[j] = vc[r].astype(vc_ref.dtype)


def _compress_sample(page_table, cache_t, perm, peab, wab, w2k, w2v, csc, *, layer, nb):
    bd, n_pages = page_table.shape
    n = n_pages * (PAGE // CMP_STRIDE)
    in_specs = ([pl.BlockSpec((1, 1, 2 * KV_W, PAGE), lambda i, pt, j=j, k=k: (layer, pt[i * nb + j, k], 0, 0))
                 for j in range(nb) for k in range(n_pages)]
                + [pl.BlockSpec((PAGE, PAGE), lambda i, pt: (0, 0)),
                   pl.BlockSpec((SUBLANES, 2048), lambda i, pt: (0, 0)),
                   pl.BlockSpec((2, 2, 2048, 256), lambda i, pt: (0, 0, 0, 0), pipeline_mode=pl.Buffered(1)),
                   pl.BlockSpec((256, 256), lambda i, pt: (0, 0)),
                   pl.BlockSpec((256, 128), lambda i, pt: (0, 0)),
                   pl.BlockSpec((n, 256), lambda i, pt: (0, 0))])
    out = pl.BlockSpec((nb, n, LANES), lambda i, pt: (i, 0, 0))
    est = 2 * nb * n_pages * 2 * KV_W * PAGE * 4 + wab.size * 2 + 10 * nb * n * 2048 * 4 + (4 << 20)
    return pl.pallas_call(
        functools.partial(_cmp_sample_kernel, n_pages=n_pages, nb=nb),
        grid_spec=pltpu.PrefetchScalarGridSpec(
            num_scalar_prefetch=1,
            grid=(bd // nb,),
            in_specs=in_specs,
            out_specs=(out, out),
            scratch_shapes=[pltpu.VMEM((nb * n + SUBLANES, 2 * CMP_HIDDEN), F32),
                            pltpu.VMEM((2, SUBLANES, 2 * CMP_HIDDEN), F32)]),
        out_shape=(jax.ShapeDtypeStruct((bd, n, LANES), MXU), jax.ShapeDtypeStruct((bd, n, LANES), MXU)),
        compiler_params=_cparams(est),
        name="compress_sample",
    )(page_table, *([cache_t] * (nb * n_pages)), perm, peab, wab, w2k, w2v, csc)


def _compress_prompt(kvt, perm, peab, wab, w2k, w2vt, csc, *, layer):
    _, b, _, s = kvt.shape
    n_pages = s // PAGE
    n = s // CMP_STRIDE
    est = 2 * 256 * s * 4 + wab.size * 2 + 8 * n * 2048 * 4 + (4 << 20)
    return pl.pallas_call(
        functools.partial(_cmp_prompt_kernel, n_pages=n_pages),
        grid=(b,),
        in_specs=[pl.BlockSpec((1, 1, 2 * KV_W, s), lambda i: (layer, i, 0, 0)),
                  pl.BlockSpec((PAGE, PAGE), lambda i: (0, 0)),
                  pl.BlockSpec((SUBLANES, 2048), lambda i: (0, 0)),
                  _resident((2, 2, 2048, 256)),
                  pl.BlockSpec((256, 256), lambda i: (0, 0)),
                  pl.BlockSpec((128, 256), lambda i: (0, 0)),
                  pl.BlockSpec((n, 256), lambda i: (0, 0))],
        out_specs=(pl.BlockSpec((1, n, LANES), lambda i: (i, 0, 0)),
                   pl.BlockSpec((1, LANES, n), lambda i: (i, 0, 0))),
        out_shape=(jax.ShapeDtypeStruct((b, n, LANES), MXU), jax.ShapeDtypeStruct((b, LANES, n), MXU)),
        scratch_shapes=[pltpu.VMEM((n + SUBLANES, 2 * CMP_HIDDEN), F32)],
        compiler_params=_cparams(est),
        name="compress_prompt",
    )(kvt, perm, peab, wab, w2k, w2vt, csc)


def _attn_prompt_kernel(qt_ref, kc_ref, vct_ref, krows_ref, va_ref, gt_ref, cov_ref, oh_ref,
                        y_ref, qaug, score_scr, *, n_cmp, n_sel):
    qb = pl.program_id(1)
    q0 = qb * Q_BLOCK
    nq = N_HEADS * Q_BLOCK
    half = Q_PER_KV * Q_BLOCK

    qt = qt_ref[0]
    rsel = lax.broadcasted_iota(jnp.int32, (2 * HEAD_DIM, 1), 0) < HEAD_DIM
    zero = jnp.zeros((), qt.dtype)
    for g in range(N_KV):
        for r in range(Q_PER_KV):
            blk = qt[r * LANES:(r + 1) * LANES, :]
            keep = rsel if g == 0 else jnp.logical_not(rsel)
            qaug[0:LANES, (g * Q_PER_KV + r) * Q_BLOCK:(g * Q_PER_KV + r + 1) * Q_BLOCK] = jnp.where(keep, blk, zero)
    qbd = qaug[0:LANES, :]

    lane = lax.broadcasted_iota(jnp.int32, (1, nq), 1)
    qpos = q0 + jnp.bitwise_and(lane, Q_BLOCK - 1)

    sc = _dot(kc_ref[0], qbd)
    posc = lax.broadcasted_iota(jnp.int32, (n_cmp, 1), 0) * CMP_STRIDE + (CMP_LEN - 1)
    valid = posc <= qpos
    s = jnp.where(valid, sc, NEG)
    mx = jnp.max(s, axis=0, keepdims=True)
    mx = jnp.where(mx > 0.5 * NEG, mx, 0.0)
    e = jnp.where(valid, jnp.exp2(s - mx), 0.0)
    p = e / jnp.maximum(jnp.sum(e, axis=0, keepdims=True), 1.0)
    pm = p.astype(MXU)
    vct = vct_ref[0]
    o_cmp = [_dot(vct[g * HEAD_DIM:(g + 1) * HEAD_DIM, :], pm[:, g * half:(g + 1) * half]) for g in range(N_KV)]

    w2 = N_KV * Q_BLOCK
    jrow = lax.broadcasted_iota(jnp.int32, (n_sel, 1), 0)
    qp2 = q0 + jnp.bitwise_and(lax.broadcasted_iota(jnp.int32, (1, w2), 1), Q_BLOCK - 1)
    cur = lax.shift_right_logical(qp2, 6)
    forced = (jrow == 0) | ((jrow <= cur) & (jrow > cur - N_LOCAL))
    causal = jrow * SEL_BLOCK <= qp2
    psum = []
    for g in range(N_KV):
        ps = p[:, g * half:g * half + Q_BLOCK]
        for r in range(1, Q_PER_KV):
            ps = ps + p[:, g * half + r * Q_BLOCK:g * half + (r + 1) * Q_BLOCK]
        psum.append(ps)
    imp = None
    for term in _split3(jnp.concatenate(psum, axis=1)):
        t = _dot(cov_ref[...], term)
        imp = t if imp is None else imp + t
    score = jnp.where(forced, FORCED_SCORE, jnp.where(causal, imp, -1.0))
    score_scr[...] = score
    n_live = jnp.minimum(n_sel, 2 * qb + 2)

    rank_unroll = 4

    bits = pltpu.bitcast(score, jnp.int32)
    below = jnp.where(score > 0.0, pltpu.bitcast(bits - 1, F32),
                      jnp.where(score == 0.0, -1.1754944e-38, pltpu.bitcast(bits + 1, F32)))

    def rank_body(i, cnt):
        for u in range(rank_unroll):
            jp = i * rank_unroll + u
            rowv = score_scr[pl.ds(jp, 1), :]
            cnt = cnt + jnp.where(rowv > jnp.where(jp < jrow, below, score), 1.0, 0.0)
        return cnt

    cnt = lax.fori_loop(0, (n_live + rank_unroll - 1) // rank_unroll, rank_body, jnp.zeros((n_sel, w2), F32))
    bias = jnp.where((cnt < float(SEL_TOPK)) & causal, 0.0, NEG).astype(MXU)
    qaug[LANES:2 * LANES, :] = jnp.zeros((LANES, nq), MXU)
    qaug[LANES:LANES + n_sel, :] = jnp.concatenate(
        [bias[:, 0:Q_BLOCK]] * Q_PER_KV + [bias[:, Q_BLOCK:w2]] * Q_PER_KV, axis=1)

    def step(state, k0, kk, rhs, vrow0, visible):
        m, a0, a1 = state
        nk = kk.shape[0]
        sij = _dot(kk, rhs)
        if visible is not None:
            kp = k0 + lax.broadcasted_iota(jnp.int32, (nk, 1), 0)
            sij = jnp.where(visible(kp), sij, NEG)
        m_new = jnp.maximum(m, jnp.max(sij, axis=0, keepdims=True))
        alpha = jnp.exp2(m - m_new)
        pij = jnp.exp2(sij - m_new).astype(MXU)
        vblk = jnp.concatenate([va_ref[k0 // LANES + sub][vrow0:vrow0 + 2 * VA_ROWS, :]
                                for sub in range(nk // LANES)], axis=1)
        a0 = alpha[:, 0:half] * a0 + _dot(vblk[0:VA_ROWS], pij[:, 0:half])
        a1 = alpha[:, half:nq] * a1 + _dot(vblk[VA_ROWS:2 * VA_ROWS], pij[:, half:nq])
        return m_new, a0, a1

    def combine(sa, sb):
        ma, a0a, a1a = sa
        mb, a0b, a1b = sb
        mm = jnp.maximum(ma, mb)
        wa = jnp.exp2(ma - mm)
        wb = jnp.exp2(mb - mm)
        return (mm, wa[:, 0:half] * a0a + wb[:, 0:half] * a0b, wa[:, half:nq] * a1a + wb[:, half:nq] * a1b)

    def merge(*states):
        st = states[0]
        for other in states[1:]:
            st = combine(st, other)
        return [a[0:HEAD_DIM] * (1.0 / a[HEAD_DIM:HEAD_DIM + 1]) for a in st[1:]]

    init = (jnp.full((1, nq), NEG, F32), jnp.zeros((VA_ROWS, half), F32), jnp.zeros((VA_ROWS, half), F32))

    sc_keys = 2 * Q_BLOCK

    def sel_step(state, c, causal_mask):
        k0 = pl.multiple_of(c * sc_keys, sc_keys)
        kk = jnp.concatenate([krows_ref[pl.ds(k0, sc_keys), 0:KV_W], oh_ref[pl.ds(k0, sc_keys), :]], axis=1)
        return step(state, k0, kk, qaug[...], 0, (lambda kp: kp <= qpos) if causal_mask else None)

    n_pairs = qb // 4
    sa, sb = lax.fori_loop(0, n_pairs,
                           lambda i, st: (sel_step(st[0], 2 * i, False), sel_step(st[1], 2 * i + 1, False)),
                           (init, init))
    sb = lax.cond((qb // 2) % 2 == 1, lambda st: sel_step(st, 2 * n_pairs + 1, True), lambda st: st, sb)
    sel_o = merge(sel_step(sa, 2 * n_pairs, True), sb)

    n_back = WINDOW // Q_BLOCK

    def win_step(state, c):
        blk = qb - n_back + c
        ok = blk >= 0
        k0 = pl.multiple_of(jnp.maximum(blk, 0) * Q_BLOCK, Q_BLOCK)
        kk = krows_ref[pl.ds(k0, Q_BLOCK), KV_W:2 * KV_W]
        return step(state, k0, kk, qbd, 2 * VA_ROWS, lambda kp: (kp <= qpos) & (kp >= qpos - WINDOW) & ok)

    n_win_states = 2
    wst = [init] * n_win_states
    for c in range(n_back + 1):
        wst[c % n_win_states] = win_step(wst[c % n_win_states], c)
    win_o = merge(*wst)

    gt = gt_ref[0]
    for r in range(Q_PER_KV):
        c = slice(r * Q_BLOCK, (r + 1) * Q_BLOCK)
        pair = []
        for g in range(N_KV):
            hh = g * Q_PER_KV + r
            pair.append(o_cmp[g][:, c] * gt[hh:hh + 1, :] + sel_o[g][:, c] * gt[N_HEADS + hh:N_HEADS + hh + 1, :]
                        + win_o[g][:, c] * gt[2 * N_HEADS + hh:2 * N_HEADS + hh + 1, :])
        y_ref[:, r * LANES:(r + 1) * LANES] = jnp.concatenate(pair, axis=0).T.astype(y_ref.dtype)


def _attention_prompt(qt, kc, vct, krows, va, gt, cov, onehot, *, b, s):
    nqb = s // Q_BLOCK
    n_cmp = kc.shape[1]
    n_sel = s // SEL_BLOCK
    est = (2 * s * 2 * KV_W * 2 + 2 * (s // LANES) * 4 * VA_ROWS * LANES * 2 + s * LANES * 2 + 10 * 256 * 1024 * 4
           + 5 * n_cmp * 1024 * 4 + (6 << 20))
    return pl.pallas_call(
        functools.partial(_attn_prompt_kernel, n_cmp=n_cmp, n_sel=n_sel),
        grid=(b, nqb),
        in_specs=[pl.BlockSpec((1, ATT_W, LANES), lambda i, j: (i * nqb + j, 0, 0)),
                  pl.BlockSpec((1, n_cmp, LANES), lambda i, j: (i, 0, 0)),
                  pl.BlockSpec((1, LANES, n_cmp), lambda i, j: (i, 0, 0)),
                  pl.BlockSpec((s, 2 * KV_W), lambda i, j: (i, 0)),
                  pl.BlockSpec((s // LANES, 4 * VA_ROWS, LANES), lambda i, j: (i, 0, 0)),
                  pl.BlockSpec((1, LANES, LANES), lambda i, j: (i * nqb + j, 0, 0)),
                  pl.BlockSpec((n_sel, n_cmp), lambda i, j: (0, 0)),
                  _resident((s, LANES))],
        out_specs=pl.BlockSpec((Q_BLOCK, ATT_W), lambda i, j: (i * nqb + j, 0)),
        out_shape=jax.ShapeDtypeStruct((b * s, ATT_W), MXU),
        scratch_shapes=[pltpu.VMEM((2 * LANES, N_HEADS * Q_BLOCK), MXU), pltpu.VMEM((n_sel, N_KV * Q_BLOCK), F32)],
        compiler_params=_cparams(est, 2),
        name="attention_prompt",
    )(qt, kc, vct, krows, va, gt, cov, onehot)


def _attn_sample_kernel(pt_ref, *refs, n_pages, ns, n_new, pos0, n_sel, win_past):
    del pt_ref
    page_refs = [refs[j * n_pages:(j + 1) * n_pages] for j in range(ns)]
    zrow_ref, wnewt_ref, wpast_ref, kc_ref, vc_ref, cov_ref, exp_ref = refs[ns * n_pages:ns * n_pages + 7]
    o_ref, wout_ref = refs[-2:]
    n = n_pages * (PAGE // CMP_STRIDE)
    past = n_pages * PAGE
    grp = Q_PER_KV * SUBLANES
    per_seq = N_KV * grp
    rows = ns * per_seq
    g_off = ATT_W + 6 * KV_W

    lane_h = lax.broadcasted_iota(jnp.int32, (1, KV_W), 1)
    head_lanes = [(lane_h >= g * HEAD_DIM) & (lane_h < (g + 1) * HEAD_DIM) for g in range(N_KV)]
    zrows = [zrow_ref[j] for j in range(ns)]

    def per_row(fn):
        return jnp.concatenate([fn(j, g, r) for j in range(ns) for g in range(N_KV) for r in range(Q_PER_KV)], axis=0)

    def per_seq_rows(fn):
        return jnp.concatenate([jnp.broadcast_to(fn(j), (per_seq, fn(j).shape[1])) for j in range(ns)], axis=0)

    def seq_matmul(fn):
        return jnp.concatenate([fn(j, slice(j * per_seq, (j + 1) * per_seq)) for j in range(ns)], axis=0)

    qf = per_row(lambda j, g, r: jnp.where(head_lanes[g], zrows[j][:, r * LANES:(r + 1) * LANES], 0.0))
    q = qf.astype(MXU)
    gate = [per_row(lambda j, g, r, k=k: zrows[j][:, g_off + k * N_HEADS + g * Q_PER_KV + r:
                                                  g_off + k * N_HEADS + g * Q_PER_KV + r + 1]) for k in range(3)]

    step = jnp.bitwise_and(lax.broadcasted_iota(jnp.int32, (rows, 1), 0), SUBLANES - 1)
    qp = pos0 + step
    lane_c = lax.broadcasted_iota(jnp.int32, (1, n), 1)
    valid_c = lane_c * CMP_STRIDE + (CMP_LEN - 1) <= qp
    wl = lax.broadcasted_iota(jnp.int32, (1, win_past), 1)
    wpos = pos0 - win_past + wl
    valid_w = (wpos >= 0) & (wpos <= qp) & (wpos >= qp - WINDOW)

    sc = seq_matmul(lambda j, r: _dot_nt(q[r], kc_ref[j]))
    s = jnp.where(valid_c, sc, NEG)
    mx = jnp.max(s, axis=1, keepdims=True)
    mx = jnp.where(mx > 0.5 * NEG, mx, 0.0)
    e = jnp.where(valid_c, jnp.exp2(s - mx), 0.0)
    p = e / jnp.maximum(jnp.sum(e, axis=1, keepdims=True), 1.0)
    pm = p.astype(MXU)
    o_c = seq_matmul(lambda j, r: _dot(pm[r], vc_ref[j]))

    n_grp = ns * N_KV
    psum = jnp.concatenate([sum(p[gi * grp + r * SUBLANES:gi * grp + (r + 1) * SUBLANES] for r in range(Q_PER_KV))
                            for gi in range(n_grp)], axis=0)
    imp = _dot3(psum, cov_ref[...])
    jl = lax.broadcasted_iota(jnp.int32, (1, LANES), 1)
    qp8 = pos0 + jnp.bitwise_and(lax.broadcasted_iota(jnp.int32, (n_grp * SUBLANES, 1), 0), SUBLANES - 1)
    cur = lax.shift_right_logical(qp8, 6)
    forced = (jl == 0) | ((jl <= cur) & (jl > cur - N_LOCAL))
    causal = (jl * SEL_BLOCK <= qp8) & (jl < n_sel)
    score = jnp.where(forced, FORCED_SCORE, jnp.where(causal, imp, -1.0))
    cnt = jnp.zeros_like(score)
    for jp in range(n_sel):
        col = score[:, jp:jp + 1]
        cnt = cnt + jnp.where((col > score) | ((col == score) & (jp < jl)), 1.0, 0.0)
    sel8 = ((cnt < float(SEL_TOPK)) & causal).astype(F32)
    sel = jnp.concatenate([sel8[gi * SUBLANES:(gi + 1) * SUBLANES] for gi in range(n_grp) for _ in range(Q_PER_KV)],
                          axis=0)
    keymask = _dot(sel.astype(MXU), exp_ref[...])

    def new_keys(lo, ok_fn):
        out = []
        for tk in range(n_new):
            krow = per_seq_rows(lambda j: zrows[j][tk:tk + 1, lo:lo + KV_W])
            vrow = per_seq_rows(lambda j: zrows[j][tk:tk + 1, lo + KV_W:lo + 2 * KV_W])
            sv = jnp.sum(qf * krow, axis=1, keepdims=True)
            out.append((jnp.where(ok_fn(tk), sv, NEG), vrow))
        return out

    def softmax_two(s_past, extra, v_of):
        mx = jnp.max(s_past, axis=1, keepdims=True)
        for sv, _ in extra:
            mx = jnp.maximum(mx, sv)
        pp = jnp.exp2(s_past - mx)
        den = jnp.sum(pp, axis=1, keepdims=True)
        pb = pp.astype(MXU)
        acc = seq_matmul(lambda j, r: _dot_nt(pb[r], v_of(j)))
        for sv, vv in extra:
            pe = jnp.exp2(sv - mx)
            den = den + pe
            acc = acc + pe * vv
        return acc / den

    def sel_rows(j, lo):
        return jnp.concatenate([pr[0, 0, lo:lo + KV_W, :] for pr in page_refs[j]], axis=1).astype(MXU)

    s_sel = jnp.where(keymask > 0.5, seq_matmul(lambda j, r: _dot(q[r], sel_rows(j, 0))), NEG)
    new_blk = past // SEL_BLOCK
    sel_new = sel[:, new_blk:new_blk + 1] > 0.5
    extra = new_keys(ATT_W + 2 * KV_W, lambda tk: sel_new & (tk <= step))
    o_s = softmax_two(s_sel, extra, lambda j: sel_rows(j, KV_W))

    wpast = [wpast_ref[0, j] for j in range(ns)]
    s_win = jnp.where(valid_w, seq_matmul(lambda j, r: _dot(q[r], wpast[j][0:KV_W].astype(MXU))), NEG)
    extra = new_keys(ATT_W + 4 * KV_W, lambda tk: tk <= step)
    o_w = softmax_two(s_win, extra, lambda j: wpast[j][KV_W:2 * KV_W].astype(MXU))

    o = o_c * gate[0] + o_s * gate[1] + o_w * gate[2]
    for j in range(ns):
        base = j * per_seq
        o_ref[j] = jnp.where(head_lanes[0], o[base:base + grp], o[base + grp:base + 2 * grp])
        rolled = pltpu.roll(wpast[j], win_past - n_new, 1)
        newr = pltpu.roll(wnewt_ref[j], LANES - n_new, 1)
        wout_ref[0, j, :, 0:win_past - LANES] = rolled[:, 0:win_past - LANES]
        wout_ref[0, j, :, win_past - LANES:win_past] = jnp.where(jl >= LANES - n_new, newr,
                                                                 rolled[:, win_past - LANES:win_past])


def _attention_sample(page_table, cache_t, zrow, wnewt, wpast_t, kc, vc, cov, expand, win_prev,
                      *, layer, ns, pos0, n_new):
    bd, n_pages = page_table.shape
    n = n_pages * (PAGE // CMP_STRIDE)
    past = n_pages * PAGE
    n_sel = -(-(past + n_new) // SEL_BLOCK)
    win_past = wpast_t.shape[3]
    rows = Q_PER_KV * SUBLANES

    in_specs = ([pl.BlockSpec((1, 1, 2 * KV_W, PAGE), lambda i, pt, j=j, k=k: (layer, pt[i * ns + j, k], 1, 0))
                 for j in range(ns) for k in range(n_pages)]
                + [pl.BlockSpec((ns, SUBLANES, Z_ALL), lambda i, pt: (i, 0, 0)),
                   pl.BlockSpec((ns, 2 * KV_W, LANES), lambda i, pt: (i, 0, 0)),
                   pl.BlockSpec((1, ns, 2 * KV_W, win_past), lambda i, pt: (layer, i, 0, 0)),
                   pl.BlockSpec((ns, n, LANES), lambda i, pt: (i, 0, 0)),
                   pl.BlockSpec((ns, n, LANES), lambda i, pt: (i, 0, 0)),
                   pl.BlockSpec((n, LANES), lambda i, pt: (0, 0)),
                   pl.BlockSpec((LANES, past), lambda i, pt: (0, 0))])
    est = (2 * ns * n_pages * 2 * KV_W * PAGE * 4 + ns * 8 * past * KV_W * 4 + 8 * ns * 2 * KV_W * win_past * 4
           + (6 << 20))
    args = [page_table, *([cache_t] * (ns * n_pages)), zrow, wnewt, wpast_t, kc, vc, cov, expand]
    aliases = {}
    if win_prev is not None:
        aliases = {len(args): 1}
        in_specs.append(pl.BlockSpec(memory_space=pl.ANY))
        args.append(win_prev)
    return pl.pallas_call(
        functools.partial(_attn_sample_kernel, n_pages=n_pages, ns=ns, n_new=n_new, pos0=pos0, n_sel=n_sel,
                          win_past=win_past),
        grid_spec=pltpu.PrefetchScalarGridSpec(
            num_scalar_prefetch=1,
            grid=(bd // ns,),
            in_specs=in_specs,
            out_specs=(pl.BlockSpec((ns, rows, KV_W), lambda i, pt: (i, 0, 0)),
                       pl.BlockSpec((1, ns, 2 * KV_W, win_past), lambda i, pt: (layer, i, 0, 0)))),
        out_shape=(jax.ShapeDtypeStruct((bd, rows, KV_W), F32),
                   jax.ShapeDtypeStruct(wpast_t.shape, F32)),
        input_output_aliases=aliases,
        compiler_params=_cparams(est),
        name="attention_sample",
    )(*args)


def _merge_kernel(x_ref, yp_ref, yn_ref, yc_ref, sg_ref, wbp_ref, wbn_ref, wbc_ref, wo_ref, o_ref):
    d = D_MODEL
    merged = (sg_ref[:, 0:d].astype(F32) * _dot(yp_ref[...], wbp_ref[...])
              + sg_ref[:, d:2 * d].astype(F32) * _dot(yn_ref[...], wbn_ref[...])
              + sg_ref[:, 2 * d:3 * d].astype(F32) * _dot(yc_ref[...], wbc_ref[...]))
    o_ref[...] = x_ref[...] + _dot(merged.astype(MXU), wo_ref[...])


def _merge(x, yp, yn, yc, sg, wbp, wbn, wbc, wo, *, tm):
    m = x.shape[0]
    row = lambda w: pl.BlockSpec((tm, w), lambda i: (i, 0))
    est = 2 * tm * (2 * D_MODEL * 4 + 1024 * 2 + 3072 * 2) + 2 * 3 * D_MODEL * D_MODEL + 6 * tm * D_MODEL * 4 + (2 << 20)
    return pl.pallas_call(
        _merge_kernel,
        grid=(m // tm,),
        in_specs=[row(D_MODEL), row(POOL_W), row(ATT_W), row(CONV_DIM), row(3 * D_MODEL),
                  _resident((POOL_W, D_MODEL)), _resident((ATT_W, D_MODEL)), _resident((CONV_DIM, D_MODEL)),
                  _resident((D_MODEL, D_MODEL))],
        out_specs=row(D_MODEL),
        out_shape=jax.ShapeDtypeStruct((m, D_MODEL), F32),
        compiler_params=_cparams(est),
        name="merge",
    )(x, yp, yn, yc, sg, wbp, wbn, wbc, wo)


def _ffn_kernel(x_ref, g_ref, pre_ref, wup_ref, wcv_ref, wdn_ref, gf_ref, o_ref, last_ref, *rest,
                tm, stride, tiles_per_seq, final):
    yf_ref = rest[0] if final else None
    aext = rest[-1]
    i = pl.program_id(0)
    hr = aext.shape[0] - tm

    @pl.when(i % tiles_per_seq == 0)
    def _():
        aext[0:hr, :] = pre_ref[...]

    x = x_ref[...]
    h = x * lax.rsqrt(jnp.mean(x * x, axis=-1, keepdims=True) + EPS)
    h = (h * g_ref[...]).astype(MXU)
    acc = x
    for j in range(D_FF // FF_CHUNK):
        cj = slice(j * FF_CHUNK, (j + 1) * FF_CHUNK)
        a = _dot(h, wup_ref[:, cj])
        aext[hr:hr + tm, cj] = a
        a1 = aext[hr - stride:hr - stride + tm, cj]
        a2 = aext[hr - 2 * stride:hr - 2 * stride + tm, cj]
        ac = wcv_ref[0:1, cj] * a2 + wcv_ref[1:2, cj] * a1 + wcv_ref[2:3, cj] * a
        bgate = _dot(h, wup_ref[:, D_FF + j * FF_CHUNK:D_FF + (j + 1) * FF_CHUNK])
        gated = (ac * _sigmoid(ac) * bgate).astype(MXU)
        acc = acc + _dot(gated, wdn_ref[cj, :])
    o_ref[...] = acc
    if final:
        yf = acc * lax.rsqrt(jnp.mean(acc * acc, axis=-1, keepdims=True) + EPS)
        yf_ref[...] = yf * gf_ref[...]
    keep = (CONV_TAPS - 1) * stride
    last_ref[0] = aext[hr + tm - keep:hr + tm, :]
    if tiles_per_seq > 1:
        aext[0:hr, :] = aext[tm:tm + hr, :]


def _conv_ffn(x, g, pre, wup, wcv, wdn, gf, *, tm, stride, tiles_per_seq, final):
    m = x.shape[0]
    nt = m // tm
    nseq = nt // tiles_per_seq
    hr = pre.shape[0] // nseq
    keep = (CONV_TAPS - 1) * stride
    row = pl.BlockSpec((tm, D_MODEL), lambda i: (i, 0))
    est = (3 * 2 * tm * D_MODEL * 4 + wup.size * 2 + wdn.size * 2 + (hr + tm) * D_FF * 4 + 2 * hr * D_FF * 4
           + 2 * keep * D_FF * 4 + 8 * tm * D_MODEL * 4 + (2 << 20))
    out_specs = [row, pl.BlockSpec((1, keep, D_FF), lambda i: (i // tiles_per_seq, 0, 0))]
    out_shape = [jax.ShapeDtypeStruct((m, D_MODEL), F32), jax.ShapeDtypeStruct((nseq, keep, D_FF), F32)]
    if final:
        out_specs.append(row)
        out_shape.append(jax.ShapeDtypeStruct((m, D_MODEL), F32))
    res = pl.pallas_call(
        functools.partial(_ffn_kernel, tm=tm, stride=stride, tiles_per_seq=tiles_per_seq, final=final),
        grid=(nt,),
        in_specs=[row, pl.BlockSpec((1, D_MODEL), lambda i: (0, 0)),
                  pl.BlockSpec((hr, D_FF), lambda i: (i // tiles_per_seq, 0)),
                  _resident((D_MODEL, 2 * D_FF)), pl.BlockSpec((CONV_TAPS, D_FF), lambda i: (0, 0)),
                  _resident((D_FF, D_MODEL)), pl.BlockSpec((1, D_MODEL), lambda i: (0, 0))],
        out_specs=tuple(out_specs),
        out_shape=tuple(out_shape),
        scratch_shapes=[pltpu.VMEM((hr + tm, D_FF), F32)],
        compiler_params=_cparams(est),
        name="conv_ffn",
    )(x, g, pre, wup, wcv, wdn, gf)
    return (res[0], res[1], res[2] if final else None)


def _rot_cols(w):
    k = w.shape[0]
    w4 = w.reshape(k, -1, 2, HEAD_DIM // 2)
    return jnp.stack([-w4[:, :, 1], w4[:, :, 0]], axis=2).reshape(k, -1)


def _rope_table_t(pos):
    half = HEAD_DIM // 2
    inv = ROPE_THETA ** (-jnp.arange(half, dtype=F32) / half)
    ang = pos.astype(F32)[:, None] * inv[None, :]
    return jnp.concatenate([jnp.tile(jnp.cos(ang), (1, 4)), jnp.tile(jnp.sin(ang), (1, 4))], axis=1).T


def _blockdiag2(w):
    z = jnp.zeros_like(w)
    return jnp.concatenate([jnp.concatenate([w, z], axis=1), jnp.concatenate([z, w], axis=1)], axis=0)


def _head_major_to_pair_major(w, axis):
    shp = w.shape
    w = w.reshape(shp[:axis] + (N_KV, Q_PER_KV, HEAD_DIM) + shp[axis + 1:])
    w = jnp.swapaxes(w, axis, axis + 1)
    return w.reshape(shp)


def _layer_weights(w_in, pool_w, cmp_pe, cmp_w1, cmp_w2, w_br_nsa):
    parts = []
    off = 0
    for w in IN_SPLITS:
        parts.append(w_in[:, off:off + w])
        off += w
    u, q, kv, gate, cb, cc, cx, gp, gn, gc = parts
    kv6 = kv.reshape(D_MODEL, 6, KV_W)
    kcmp, vcmp, ksel, vsel, kwin, vwin = [kv6[:, s] for s in range(6)]
    wrow = jnp.concatenate([u, cb, cc, cx, gp, gn, gc], axis=1).astype(MXU)
    qs = _head_major_to_pair_major(q, 1) * (HEAD_DIM ** -0.5 * LOG2E)
    gpad = jnp.pad(gate, ((0, 0), (0, LANES - gate.shape[1])))
    wt = jnp.concatenate([qs, _rot_cols(qs), kcmp, vcmp, ksel, _rot_cols(ksel), vsel,
                          kwin, _rot_cols(kwin), vwin, gpad], axis=1).T.astype(MXU)
    wbn = _head_major_to_pair_major(w_br_nsa, 0).astype(MXU)

    n_grp = len(POOL_WINDOWS)
    same_grp = jnp.eye(n_grp, dtype=bool)[:, None, :, None]
    pw = jnp.where(same_grp, pool_w[:, :, None, :], 0.0).reshape(POOL_W, POOL_W)

    wab = []
    peab = []
    for s in range(2):
        w1r = cmp_w1[s].reshape(CMP_LEN, HEAD_DIM, CMP_HIDDEN)
        halves = []
        for hsel in range(2):
            wh = w1r[hsel * CMP_STRIDE:(hsel + 1) * CMP_STRIDE]
            same_head = jnp.eye(N_KV, dtype=bool)[None, :, None, :, None]
            z = jnp.where(same_head, wh[:, None, :, None, :], 0.0)
            halves.append(z.reshape(CMP_STRIDE * KV_W, N_KV * CMP_HIDDEN))
            pe = cmp_pe[s][hsel * CMP_STRIDE:(hsel + 1) * CMP_STRIDE]
            peab.append(jnp.broadcast_to(pe[:, None, :], (CMP_STRIDE, N_KV, HEAD_DIM)).reshape(1, -1))
        wab.append(jnp.stack(halves))
    wab = jnp.stack(wab).astype(MXU)
    peab = jnp.concatenate(peab + [jnp.zeros((SUBLANES - 4, CMP_STRIDE * KV_W), F32)], axis=0)
    w2k = jnp.concatenate([_blockdiag2(cmp_w2[0]), _blockdiag2(_rot_cols(cmp_w2[0]))], axis=1).astype(MXU)
    w2v = _blockdiag2(cmp_w2[1]).astype(MXU)
    return dict(wrow=wrow, wt=wt, wbn=wbn, pw=pw.astype(MXU), wab=wab, peab=peab, w2k=w2k, w2v=w2v, w2vt=w2v.T)


def _cover(n_cmp, n_sel):
    c_start = jnp.arange(n_cmp) * CMP_STRIDE
    j_start = jnp.arange(n_sel) * SEL_BLOCK
    return ((c_start[:, None] < j_start[None, :] + SEL_BLOCK)
            & (c_start[:, None] + CMP_LEN > j_start[None, :])).astype(MXU)


def _feature_major(x):
    nd = x.ndim
    perm = tuple(range(nd - 4)) + (nd - 3, nd - 2, nd - 1, nd - 4)
    xt = jnp.transpose(x, perm)
    return xt.reshape(xt.shape[:nd - 4] + (-1, xt.shape[-1]))


def _token_major(xt, a, b, c):
    nd = xt.ndim
    x = xt.reshape(xt.shape[:nd - 2] + (a, b, c, xt.shape[-1]))
    perm = tuple(range(nd - 2)) + (nd + 1, nd - 2, nd - 1, nd)
    return jnp.transpose(x, perm)


def kernel(x_prompt, x_sample, cache_kv, cache_win, state_pool, state_conv, state_ffn, page_table,
           norm_mix, w_in, pool_w, pool_scale, cmp_pe, cmp_w1, cmp_w2, conv_w,
           w_br_pool, w_br_nsa, w_br_conv, w_out, norm_ffn, ffn_up, ffn_conv, ffn_down, norm_final):
    bp, sp, d = x_prompt.shape
    bd, sd, _ = x_sample.shape
    depth = w_in.shape[0]
    n_pages = page_table.shape[1]
    past = n_pages * PAGE
    wpast = cache_win.shape[2]
    assert d == D_MODEL and sp % (4 * Q_BLOCK) == 0 and sp >= WINDOW and sp // SEL_BLOCK <= LANES
    assert bd == LANES and sd & (sd - 1) == 0 and sd <= SUBLANES
    assert past % SEL_BLOCK == 0 and cache_kv.shape[2] == PAGE and wpast == WINDOW

    tm_p = 512
    mp = bp * sp
    md = bd * sd
    tps = sp // tm_p

    cst_p = _rope_table_t(jnp.arange(sp))
    cst_d = _rope_table_t(jnp.repeat(past + jnp.arange(sd), bd))
    nc_p = sp // CMP_STRIDE
    nc_d = past // CMP_STRIDE
    csc_p = _rope_table_t(jnp.arange(nc_p) * CMP_STRIDE + CMP_LEN - 1).T
    csc_d = _rope_table_t(jnp.arange(nc_d) * CMP_STRIDE + CMP_LEN - 1).T
    nsel_p = sp // SEL_BLOCK
    cov_pt = _cover(nc_p, nsel_p).T
    cov_d = _cover(nc_d, LANES)
    expand = (jnp.arange(LANES)[:, None] == (jnp.arange(past) // SEL_BLOCK)[None, :]).astype(MXU)
    onehot_p = ((jnp.arange(sp) // SEL_BLOCK)[:, None] == jnp.arange(LANES)[None, :]).astype(MXU)
    tok = jnp.arange(PAGE)
    perm = (tok[None, :] == (tok[:, None] % (PAGE // CMP_STRIDE)) * CMP_STRIDE + tok[:, None] // (PAGE // CMP_STRIDE))
    perm = perm.astype(MXU)

    hr_pool_p = 2 * SUBLANES
    hr_conv_p = SUBLANES
    zeros_pool = jnp.zeros((bp * hr_pool_p, POOL_W), F32)
    zeros_conv = jnp.zeros((bp * hr_conv_p, CONV_DIM), F32)
    zeros_ffn = jnp.zeros((bp * hr_conv_p, D_FF), F32)

    cache_t = _feature_major(cache_kv)
    win_t = _feature_major(cache_win)
    pool_tm = jnp.transpose(state_pool, (0, 2, 1, 3))

    xp = x_prompt.reshape(mp, d)
    xs = jnp.transpose(x_sample, (1, 0, 2)).reshape(md, d)

    keys = ("kv_s", "win_p", "pool_p", "pool_s", "conv_p", "conv_s", "ffn_p", "ffn_s")
    outs = {k: [] for k in keys}
    yp_final = ys_final = None
    kvt = win_new = None
    for l in range(depth):
        lw = _layer_weights(w_in[l], pool_w[l], cmp_pe[l], cmp_w1[l], cmp_w2[l], w_br_nsa[l])
        g_mix = norm_mix[l].reshape(1, d)
        g_ffn = norm_ffn[l].reshape(1, d)
        ps = pool_scale[l].reshape(1, POOL_W)
        cw = conv_w[l]
        wbp = w_br_pool[l].astype(MXU)
        wbc = w_br_conv[l].astype(MXU)
        wo = w_out[l].astype(MXU)
        wup = ffn_up[l].astype(MXU)
        wdn = ffn_down[l].astype(MXU)
        wcv = ffn_conv[l]
        gf = norm_final.reshape(1, d)
        final = l == depth - 1

        up, cb, ccx, sg, kvt, wint, qt, va, krows, gt = _in_proj(xp, g_mix, lw["wrow"], lw["wt"], cst_p,
                                                               tm=tm_p, seq=sp, sample=False, layer=l, depth=depth,
                                                               kv_prev=kvt)
        ypool, yconv = _mixers(up, cb, ccx, zeros_pool, zeros_conv, lw["pw"], ps, cw,
                               tm=tm_p, stride=1, tiles_per_seq=tps, pos0=0)
        kc, vct = _compress_prompt(kvt, perm, lw["peab"], lw["wab"], lw["w2k"], lw["w2vt"], csc_p, layer=l)
        ynsa = _attention_prompt(qt, kc, vct, krows, va, gt, cov_pt, onehot_p, b=bp, s=sp)
        x1 = _merge(xp, ypool, ynsa, yconv, sg, wbp, lw["wbn"], wbc, wo, tm=tm_p)
        xp, a_last, yp_final = _conv_ffn(x1, g_ffn, zeros_ffn, wup, wcv, wdn, gf, tm=tm_p, stride=1,
                                         tiles_per_seq=tps, final=final)
        outs["win_p"].append(wint[:, :, sp - WINDOW:])
        outs["pool_p"].append(up.reshape(bp, sp, POOL_W)[:, sp - (POOL_MAX - 1):])
        outs["conv_p"].append(ccx.reshape(bp, sp, CONV_DIM)[:, sp - (CONV_TAPS - 1):])
        outs["ffn_p"].append(a_last)

        up, cb, ccx, sg, zall = _in_proj(xs, g_mix, lw["wrow"], lw["wt"], cst_d, tm=md, seq=sd, sample=True)
        pool_pre = pool_tm[l].reshape((POOL_MAX - 1) * bd, POOL_W)
        conv_pre = jnp.transpose(state_conv[l], (1, 0, 2)).reshape((CONV_TAPS - 1) * bd, CONV_DIM)
        ffn_pre = jnp.transpose(state_ffn[l], (1, 0, 2)).reshape((CONV_TAPS - 1) * bd, D_FF)
        ypool, yconv = _mixers(up, cb, ccx, pool_pre, conv_pre, lw["pw"], ps, cw,
                               tm=md, stride=bd, tiles_per_seq=1, pos0=past)
        zrow = jnp.pad(jnp.transpose(zall, (2, 0, 1)), ((0, 0), (0, SUBLANES - sd), (0, 0)))
        wnew_t = zall[:, ATT_W + 4 * KV_W:ATT_W + 6 * KV_W, :]
        wnew_tp = jnp.pad(jnp.transpose(wnew_t, (2, 1, 0)), ((0, 0), (0, 0), (0, LANES - sd)))
        kc_d, vc_d = _compress_sample(page_table, cache_t, perm, lw["peab"], lw["wab"], lw["w2k"], lw["w2v"], csc_d,
                                      layer=l, nb=4)
        o_seq, win_new = _attention_sample(page_table, cache_t, zrow, wnew_tp, win_t, kc_d, vc_d, cov_d, expand,
                                           win_new, layer=l, ns=4, pos0=past, n_new=sd)
        ynsa = jnp.transpose(o_seq.reshape(bd, Q_PER_KV, SUBLANES, KV_W)[:, :, 0:sd], (2, 0, 1, 3))
        ynsa = ynsa.reshape(md, ATT_W).astype(MXU)
        x1 = _merge(xs, ypool, ynsa, yconv, sg, wbp, lw["wbn"], wbc, wo, tm=md)
        xs, a_last, ys_final = _conv_ffn(x1, g_ffn, ffn_pre, wup, wcv, wdn, gf, tm=md, stride=bd,
                                         tiles_per_seq=1, final=final)
        outs["kv_s"].append(zall[:, ATT_W:ATT_W + 4 * KV_W, :])
        up_tm = up.reshape(sd, bd, POOL_W)
        pool_full = jnp.concatenate([pool_tm[l], up_tm], axis=0)
        outs["pool_s"].append(pool_full[pool_full.shape[0] - (POOL_MAX - 1):])
        ccx_seq = jnp.transpose(ccx.reshape(sd, bd, CONV_DIM), (1, 0, 2))
        conv_full = jnp.concatenate([state_conv[l], ccx_seq], axis=1)
        outs["conv_s"].append(conv_full[:, conv_full.shape[1] - (CONV_TAPS - 1):])
        outs["ffn_s"].append(jnp.transpose(a_last.reshape(CONV_TAPS - 1, bd, D_FF), (1, 0, 2)))

    st = lambda k: jnp.stack(outs[k])
    y_prompt = yp_final.reshape(bp, sp, d)
    y_sample = jnp.transpose(ys_final.reshape(sd, bd, d), (1, 0, 2))
    kv_prompt = _token_major(kvt, 4, N_KV, HEAD_DIM)
    kv_s = st("kv_s").reshape(depth, sd, 4, N_KV, HEAD_DIM, bd)
    kv_sample = jnp.transpose(kv_s, (0, 5, 1, 2, 3, 4))
    win_prompt = _token_major(st("win_p"), 2, N_KV, HEAD_DIM)
    win_sample = _token_major(win_new, 2, N_KV, HEAD_DIM)
    pool_sample = jnp.transpose(st("pool_s"), (0, 2, 1, 3))
    return (y_prompt, y_sample, kv_prompt, kv_sample, win_prompt, win_sample, st("pool_p"), pool_sample,
            st("conv_p"), st("conv_s"), st("ffn_p"), st("ffn_s"))
```
